```python
import jax, jax.numpy as jnp
from jax import lax
import numpy as np

D_MODEL = 1024
BATCH = 32
SEQ = 2048
DEPTH = 1

N_Q_HEADS = 16
N_KV_HEADS = 4
HEAD_DIM = 64
WINDOW = 128
ATTN_BLOCK = 128
ROPE_THETA = 10000.0
LRU_WIDTH = D_MODEL
LRU_BLOCKS = 8
LRU_BLOCK_W = LRU_WIDTH // LRU_BLOCKS
CONV_WIDTH = 4
LRU_C = 8.0
LRU_MIN_RAD = 0.9
LRU_MAX_RAD = 0.999
N_EXPERTS = 32
TOP_K = 4
D_FF = D_MODEL
SWIGLU_LIMIT = 7.0
SWIGLU_ALPHA = 1.702
MOE_BLOCK = 256
EPS = 1e-6

Q_W = N_Q_HEADS * HEAD_DIM
KV_W = N_KV_HEADS * HEAD_DIM
IN_W = Q_W + 2 * KV_W + 2 * LRU_WIDTH + 2 * D_MODEL

kernel_name = "hybrid_swa_rglru_moe_block"


def rms_norm(x, g):
    xf = x.astype(jnp.float32)
    xf = xf * lax.rsqrt(jnp.mean(xf * xf, axis=-1, keepdims=True) + EPS)
    return (xf * g.astype(jnp.float32)).astype(x.dtype)


def rope(t, positions):
    inv_freq = ROPE_THETA ** (-jnp.arange(0, HEAD_DIM, 2, dtype=jnp.float32) / HEAD_DIM)
    ang = positions.astype(jnp.float32)[..., None] * inv_freq
    cos = jnp.cos(ang)[:, :, None, :]
    sin = jnp.sin(ang)[:, :, None, :]
    tf = t.astype(jnp.float32)
    t1, t2 = jnp.split(tf, 2, axis=-1)
    return jnp.concatenate([t1 * cos - t2 * sin, t2 * cos + t1 * sin], axis=-1).astype(t.dtype)


def banded_sink_attention(q, k, v, sinks):
    B, S = q.shape[0], q.shape[1]
    nb = S // ATTN_BLOCK
    G = N_Q_HEADS // N_KV_HEADS
    qb = q.reshape(B, nb, ATTN_BLOCK, N_KV_HEADS, G, HEAD_DIM)

    def windows(t):
        tp = jnp.pad(t, ((0, 0), (ATTN_BLOCK, 0), (0, 0), (0, 0)))
        prev = tp[:, :S].reshape(B, nb, ATTN_BLOCK, N_KV_HEADS, HEAD_DIM)
        cur = t.reshape(B, nb, ATTN_BLOCK, N_KV_HEADS, HEAD_DIM)
        return jnp.concatenate([prev, cur], axis=2)

    kw, vw = windows(k), windows(v)
    s = jnp.einsum("bnqhgd,bnkhd->bnhgqk", qb, kw).astype(jnp.float32) * (HEAD_DIM ** -0.5)
    blk = jnp.arange(nb)[:, None, None] * ATTN_BLOCK
    qpos = blk + jnp.arange(ATTN_BLOCK)[None, :, None]
    kpos = blk - ATTN_BLOCK + jnp.arange(2 * ATTN_BLOCK)[None, None, :]
    rel = qpos - kpos
    valid = (rel >= 0) & (rel < WINDOW) & (kpos >= 0)
    s = jnp.where(valid[None, :, None, None, :, :], s, -jnp.inf)
    sink = sinks.astype(jnp.float32).reshape(1, 1, N_KV_HEADS, G, 1, 1)
    m = jnp.maximum(jnp.max(s, axis=-1, keepdims=True), sink)
    p = jnp.exp(s - m)
    denom = jnp.sum(p, axis=-1, keepdims=True) + jnp.exp(sink - m)
    o = jnp.einsum("bnhgqk,bnkhd->bnqhgd", (p / denom).astype(v.dtype), vw)
    return o.reshape(B, S, Q_W)


def causal_depthwise_conv(x, w, b):
    S = x.shape[1]
    xp = jnp.pad(x, ((0, 0), (CONV_WIDTH - 1, 0), (0, 0)))
    out = b
    for tap in range(CONV_WIDTH):
        out = out + xp[:, tap:tap + S] * w[tap]
    return out


def rg_lru(x, w_rg, b_rg, w_ig, b_ig, lru_a):
    B, S = x.shape[0], x.shape[1]
    xb = x.reshape(B, S, LRU_BLOCKS, LRU_BLOCK_W)
    r = jax.nn.sigmoid(jnp.einsum("bsnd,nde->bsne", xb, w_rg).reshape(B, S, LRU_WIDTH) + b_rg)
    i = jax.nn.sigmoid(jnp.einsum("bsnd,nde->bsne", xb, w_ig).reshape(B, S, LRU_WIDTH) + b_ig)
    log_a = -LRU_C * r.astype(jnp.float32) * jax.nn.softplus(lru_a.astype(jnp.float32))
    a = jnp.exp(log_a)
    mult = jnp.sqrt(-jnp.expm1(2.0 * log_a))
    u = mult * (i * x).astype(jnp.float32)

    def combine(left, right):
        a1, b1 = left
        a2, b2 = right
        return a1 * a2, a2 * b1 + b2

    _, h = lax.associative_scan(combine, (a, u), axis=1)
    return h.astype(x.dtype)


def hybrid_mixer(h, positions, w_in, b_in, sinks, conv_w, conv_b, w_rg, b_rg, w_ig, b_ig, lru_a, w_out, b_out):
    B, S = h.shape[0], h.shape[1]
    proj = h @ w_in + b_in
    o1 = Q_W
    o2 = o1 + KV_W
    o3 = o2 + KV_W
    o4 = o3 + LRU_WIDTH
    o5 = o4 + LRU_WIDTH
    o6 = o5 + D_MODEL
    q, k, v, lx, ly, ga, gl = jnp.split(proj, [o1, o2, o3, o4, o5, o6], axis=-1)
    q = rope(q.reshape(B, S, N_Q_HEADS, HEAD_DIM), positions)
    k = rope(k.reshape(B, S, N_KV_HEADS, HEAD_DIM), positions)
    v = v.reshape(B, S, N_KV_HEADS, HEAD_DIM)
    attn = banded_sink_attention(q, k, v, sinks)
    lx = causal_depthwise_conv(lx, conv_w, conv_b)
    lru = rg_lru(lx, w_rg, b_rg, w_ig, b_ig, lru_a) * jax.nn.gelu(ly, approximate=True)
    merged = jax.nn.sigmoid(ga) * attn + jax.nn.sigmoid(gl) * lru
    return merged @ w_out + b_out


def moe_ffn(h, w_router, b_router, w_e1, b_e1, w_e2, b_e2):
    T, D = h.shape
    logits = h.astype(jnp.float32) @ w_router.astype(jnp.float32) + b_router.astype(jnp.float32)
    top_v, top_i = lax.top_k(logits, TOP_K)
    top_w = jax.nn.softmax(top_v, axis=-1)
    N = T * TOP_K
    flat_e = top_i.reshape(-1)
    flat_w = top_w.reshape(-1)
    flat_tok = jnp.arange(N, dtype=jnp.int32) // TOP_K
    order = jnp.argsort(flat_e, stable=True)
    se = flat_e[order]
    counts = jnp.bincount(flat_e, length=N_EXPERTS)
    starts = jnp.cumsum(counts) - counts
    pcounts = (counts + MOE_BLOCK - 1) // MOE_BLOCK * MOE_BLOCK
    pends = jnp.cumsum(pcounts)
    pstarts = pends - pcounts
    dest = pstarts[se] + (jnp.arange(N) - starts[se])
    n_blocks = N // MOE_BLOCK + N_EXPERTS
    n_slots = n_blocks * MOE_BLOCK
    blk_expert = jnp.minimum(jnp.searchsorted(pends, jnp.arange(n_blocks) * MOE_BLOCK, side="right"), N_EXPERTS - 1)
    src_tok = jnp.full((n_slots,), T, jnp.int32).at[dest].set(flat_tok[order])
    slot_w = jnp.zeros((n_slots,), jnp.float32).at[dest].set(flat_w[order])
    h_pad = jnp.concatenate([h, jnp.zeros((1, D), h.dtype)], axis=0)

    def expert_block(args):
        idx, wts, e = args
        xb = h_pad[idx]
        gu = xb @ w_e1[e] + b_e1[e]
        g, u = jnp.split(gu, 2, axis=-1)
        g = jnp.minimum(g, SWIGLU_LIMIT)
        u = jnp.clip(u, -SWIGLU_LIMIT, SWIGLU_LIMIT)
        act = g * jax.nn.sigmoid(SWIGLU_ALPHA * g) * (u + 1.0)
        out = act @ w_e2[e] + b_e2[e]
        return out * wts.astype(out.dtype)[:, None]

    yp = lax.map(expert_block, (src_tok.reshape(n_blocks, MOE_BLOCK), slot_w.reshape(n_blocks, MOE_BLOCK), blk_expert))
    y = jnp.zeros((T + 1, D), yp.dtype).at[src_tok].add(yp.reshape(n_slots, D))
    return y[:T]


def setup_inputs(seed: int = 0) -> dict:
    key = jax.random.key(seed)
    ks = jax.random.split(key, 32)
    f32 = jnp.float32
    nrm = lambda k, shape, scale: jax.random.normal(k, shape, f32) * scale
    x = nrm(ks[0], (BATCH, SEQ, D_MODEL), 1.0)
    c = nrm(ks[1], (BATCH, D_MODEL), 1.0)
    offs = jax.random.randint(ks[2], (BATCH, 1), 0, SEQ, dtype=jnp.int32)
    positions = (jnp.arange(SEQ, dtype=jnp.int32)[None, :] + offs).astype(jnp.int32)
    u = jax.random.uniform(ks[3], (DEPTH, LRU_WIDTH), f32, LRU_MIN_RAD ** 2, LRU_MAX_RAD ** 2)
    lru_a = jnp.log(jnp.expm1(-0.5 * jnp.log(u)))
    return {
        "x": x,
        "c": c,
        "positions": positions,
        "w_ada": nrm(ks[4], (DEPTH, D_MODEL, 6 * D_MODEL), 0.5 * D_MODEL ** -0.5),
        "b_ada": nrm(ks[5], (DEPTH, 6 * D_MODEL), 0.01),
        "g_pre_mix": 1.0 + nrm(ks[6], (DEPTH, D_MODEL), 0.01),
        "g_post_mix": 1.0 + nrm(ks[7], (DEPTH, D_MODEL), 0.01),
        "w_in": nrm(ks[8], (DEPTH, D_MODEL, IN_W), D_MODEL ** -0.5),
        "b_in": nrm(ks[9], (DEPTH, IN_W), 0.01),
        "sinks": nrm(ks[10], (DEPTH, N_Q_HEADS), 0.5),
        "conv_w": nrm(ks[11], (DEPTH, CONV_WIDTH, LRU_WIDTH), CONV_WIDTH ** -0.5),
        "conv_b": nrm(ks[12], (DEPTH, LRU_WIDTH), 0.01),
        "w_rg": nrm(ks[13], (DEPTH, LRU_BLOCKS, LRU_BLOCK_W, LRU_BLOCK_W), LRU_BLOCK_W ** -0.5),
        "b_rg": nrm(ks[14], (DEPTH, LRU_WIDTH), 0.01),
        "w_ig": nrm(ks[15], (DEPTH, LRU_BLOCKS, LRU_BLOCK_W, LRU_BLOCK_W), LRU_BLOCK_W ** -0.5),
        "b_ig": nrm(ks[16], (DEPTH, LRU_WIDTH), 0.01),
        "lru_a": lru_a,
        "w_out": nrm(ks[17], (DEPTH, D_MODEL, D_MODEL), D_MODEL ** -0.5),
        "b_out": nrm(ks[18], (DEPTH, D_MODEL), 0.01),
        "g_pre_ffn": 1.0 + nrm(ks[19], (DEPTH, D_MODEL), 0.01),
        "g_post_ffn": 1.0 + nrm(ks[20], (DEPTH, D_MODEL), 0.01),
        "w_router": nrm(ks[21], (DEPTH, D_MODEL, N_EXPERTS), D_MODEL ** -0.5),
        "b_router": nrm(ks[22], (DEPTH, N_EXPERTS), 0.01),
        "w_e1": nrm(ks[23], (DEPTH, N_EXPERTS, D_MODEL, 2 * D_FF), D_MODEL ** -0.5),
        "b_e1": nrm(ks[24], (DEPTH, N_EXPERTS, 2 * D_FF), 0.01),
        "w_e2": nrm(ks[25], (DEPTH, N_EXPERTS, D_FF, D_MODEL), D_FF ** -0.5),
        "b_e2": nrm(ks[26], (DEPTH, N_EXPERTS, D_MODEL), 0.01),
    }


def reference(x, c, positions, w_ada, b_ada, g_pre_mix, g_post_mix, w_in, b_in, sinks, conv_w, conv_b,
              w_rg, b_rg, w_ig, b_ig, lru_a, w_out, b_out, g_pre_ffn, g_post_ffn, w_router, b_router,
              w_e1, b_e1, w_e2, b_e2):
    B, S, D = x.shape
    for l in range(DEPTH):
        mod = (jax.nn.silu(c) @ w_ada[l] + b_ada[l])[:, None, :]
        sh1, sc1, gt1, sh2, sc2, gt2 = jnp.split(mod, 6, axis=-1)
        h = rms_norm(x, g_pre_mix[l]) * (1.0 + sc1) + sh1
        y = hybrid_mixer(h, positions, w_in[l], b_in[l], sinks[l], conv_w[l], conv_b[l], w_rg[l], b_rg[l],
                         w_ig[l], b_ig[l], lru_a[l], w_out[l], b_out[l])
        x = x + gt1 * rms_norm(y, g_post_mix[l])
        h = rms_norm(x, g_pre_ffn[l]) * (1.0 + sc2) + sh2
        y = moe_ffn(h.reshape(B * S, D), w_router[l], b_router[l], w_e1[l], b_e1[l], w_e2[l], b_e2[l]).reshape(B, S, D)
        x = x + gt2 * rms_norm(y, g_post_ffn[l])
    return x
```

```python
import functools

import jax
import jax.numpy as jnp
from jax import lax
from jax.experimental import pallas as pl
from jax.experimental.pallas import tpu as pltpu

F32 = jnp.float32
BF16 = jnp.bfloat16

N_Q_HEADS = 16
N_KV_HEADS = 4
HEAD_DIM = 64
ATTN_BLOCK = 128
ROPE_THETA = 10000.0
LRU_BLOCKS = 8
CONV_WIDTH = 4
LRU_C = 8.0
N_EXPERTS = 32
TOP_K = 4
SWIGLU_LIMIT = 7.0
SWIGLU_ALPHA = 1.702
EPS = 1e-6

LANES = 128
SUBLANES = 8
NEG_BIG = -1e30

MIX_CHUNK = 256
ROUTE_TILE = 512
MOVE_TILE = 512
COMB_TILE = 256
EXPERT_ROWS = 256
VMEM_LIMIT = 56 * 1024 * 1024


def _rms(x, g):
    ms = jnp.mean(x * x, axis=-1, keepdims=True)
    return x * lax.rsqrt(ms + EPS) * g


def _ada_kernel(c_ref, w_ref, b_ref, o_ref):
    c = c_ref[...]
    a = c * jax.nn.sigmoid(c)
    o_ref[...] = jnp.dot(a, w_ref[...], preferred_element_type=F32) + b_ref[...]


def _ada(c, w, b):
    bsz, d = c.shape
    n = w.shape[1] // d
    return pl.pallas_call(
        _ada_kernel,
        grid=(n,),
        in_specs=[
            pl.BlockSpec((bsz, d), lambda j: (0, 0)),
            pl.BlockSpec((d, d), lambda j: (0, j)),
            pl.BlockSpec((1, d), lambda j: (0, j)),
        ],
        out_specs=pl.BlockSpec((bsz, d), lambda j: (0, j)),
        out_shape=jax.ShapeDtypeStruct((bsz, w.shape[1]), F32),
        compiler_params=pltpu.CompilerParams(dimension_semantics=("arbitrary",), vmem_limit_bytes=VMEM_LIMIT),
        name="ada",
    )(c, w, b.reshape(1, -1))


def _mixer_kernel(x_ref, pos_ref, mod_ref, invf_ref, gpre_ref, gpost_ref, win_ref, bin_ref, sinks_ref,
                  convw_ref, convb_ref, wgate_ref, brg_ref, big_ref, lrua_ref, wout_ref, bout_ref,
                  o_ref, kprev, vprev, lxbuf, hcarry, abuf, hbuf, pbuf, attnbuf):
    ts, d = x_ref.shape
    qw = N_Q_HEADS * HEAD_DIM
    kvw = N_KV_HEADS * HEAD_DIM
    s_idx = pl.program_id(1)

    @pl.when(s_idx == 0)
    def _():
        kprev[...] = jnp.zeros_like(kprev)
        vprev[...] = jnp.zeros_like(vprev)
        lxbuf[0:SUBLANES, :] = jnp.zeros((SUBLANES, d), F32)
        hcarry[...] = jnp.zeros_like(hcarry)

    x = x_ref[...]
    sh1 = mod_ref[0:1, :]
    sc1 = mod_ref[1:2, :]
    gt1 = mod_ref[2:3, :]
    h = _rms(x, gpre_ref[...]) * (1.0 + sc1) + sh1
    hb = h.astype(BF16)

    def proj(lo, hi):
        return jnp.dot(hb, win_ref[:, lo:hi], preferred_element_type=F32) + bin_ref[:, lo:hi]

    ang = pos_ref[...] * invf_ref[...]
    cos = jnp.cos(ang)
    sin = jnp.sin(ang)
    lane = lax.broadcasted_iota(jnp.int32, (ts, LANES), 1)
    first_half = (lane & (HEAD_DIM // 2)) == 0
    sin_signed = jnp.where(first_half, -sin, sin)

    def rope(t):
        rot = jnp.where(first_half, pltpu.roll(t, LANES - HEAD_DIM // 2, 1), pltpu.roll(t, HEAD_DIM // 2, 1))
        return t * cos + rot * sin_signed

    q = proj(0, qw)
    scale = HEAD_DIM ** -0.5
    qb = jnp.concatenate([rope(q[:, j * LANES:(j + 1) * LANES]) * scale for j in range(qw // LANES)],
                         axis=1).astype(BF16)
    k = proj(qw, qw + kvw)
    kr = jnp.concatenate([rope(k[:, j * LANES:(j + 1) * LANES]) for j in range(kvw // LANES)], axis=1)
    v = proj(qw + kvw, qw + 2 * kvw)
    kext = jnp.concatenate([kprev[...], kr], axis=0)
    vext = jnp.concatenate([vprev[...], v], axis=0)
    kprev[...] = kr[ts - ATTN_BLOCK:, :]
    vprev[...] = v[ts - ATTN_BLOCK:, :]

    lane_e = lax.broadcasted_iota(jnp.int32, (ts + ATTN_BLOCK, LANES), 1)
    low = lane_e < HEAD_DIM

    def head_variants(t):
        lo_ = jnp.where(low, t, 0.0)
        hi_ = jnp.where(low, 0.0, t)
        lo_sw = pltpu.roll(lo_, HEAD_DIM, 1)
        hi_sw = pltpu.roll(hi_, HEAD_DIM, 1)
        return ((lo_.astype(BF16), lo_sw.astype(BF16)), (hi_sw.astype(BF16), hi_.astype(BF16)))

    kvar = []
    vvar = []
    for c in range(kvw // LANES):
        ke, ko = head_variants(kext[:, c * LANES:(c + 1) * LANES])
        ve, vo = head_variants(vext[:, c * LANES:(c + 1) * LANES])
        kvar += [ke, ko]
        vvar += [ve, vo]

    rows2 = 2 * ATTN_BLOCK
    qi = lax.broadcasted_iota(jnp.int32, (rows2, rows2), 0) & (ATTN_BLOCK - 1)
    kj = lax.broadcasted_iota(jnp.int32, (rows2, rows2), 1)
    cur_ok = (kj >= ATTN_BLOCK) & ((kj - ATTN_BLOCK) <= qi)
    prev_ok = (kj < ATTN_BLOCK) & (kj > qi)
    bias_mid = jnp.where(cur_ok | prev_ok, 0.0, NEG_BIG)
    bias_first = jnp.where(cur_ok | (prev_ok & (s_idx > 0)), 0.0, NEG_BIG)
    upper_rows = lax.broadcasted_iota(jnp.int32, (rows2, 1), 0) < ATTN_BLOCK

    group = N_Q_HEADS // N_KV_HEADS
    for blk in range(ts // ATTN_BLOCK):
        r0 = blk * ATTN_BLOCK
        bias = bias_first if blk == 0 else bias_mid
        for g in range(N_KV_HEADS):
            (k_a, k_b), (v_a, v_b) = kvar[g], vvar[g]
            qg = jnp.concatenate([qb[r0:r0 + ATTN_BLOCK, (2 * g) * LANES:(2 * g + 1) * LANES],
                                  qb[r0:r0 + ATTN_BLOCK, (2 * g + 1) * LANES:(2 * g + 2) * LANES]], axis=0)
            og = None
            for half, (kk, vv) in enumerate(((k_a, v_a), (k_b, v_b))):
                sc = lax.dot_general(qg, kk[r0:r0 + rows2, :], (((1,), (1,)), ((), ())),
                                     preferred_element_type=F32) + bias
                sink = jnp.where(upper_rows, sinks_ref[0, group * g + half], sinks_ref[0, group * g + 2 + half])
                m = jnp.maximum(jnp.max(sc, axis=1, keepdims=True), sink)
                p = jnp.exp(sc - m)
                denom = jnp.sum(p, axis=1, keepdims=True) + jnp.exp(sink - m)
                o = jnp.dot(p.astype(BF16), vv[r0:r0 + rows2, :], preferred_element_type=F32) / denom
                og = o if og is None else og + o
            attnbuf[r0:r0 + ATTN_BLOCK, (2 * g) * LANES:(2 * g + 1) * LANES] = og[:ATTN_BLOCK, :]
            attnbuf[r0:r0 + ATTN_BLOCK, (2 * g + 1) * LANES:(2 * g + 2) * LANES] = og[ATTN_BLOCK:, :]

    o3 = qw + 2 * kvw
    lxbuf[SUBLANES:SUBLANES + ts, :] = proj(o3, o3 + d)
    conv = convb_ref[...]
    for tap in range(CONV_WIDTH):
        off = SUBLANES - (CONV_WIDTH - 1) + tap
        conv = conv + lxbuf[off:off + ts, :] * convw_ref[tap:tap + 1, :]
    lxbuf[0:SUBLANES, :] = lxbuf[ts:ts + SUBLANES, :]
    cb = conv.astype(BF16)
    bw = d // LRU_BLOCKS
    rs, is_ = [], []
    for n in range(LRU_BLOCKS):
        ri = jnp.dot(cb[:, n * bw:(n + 1) * bw], wgate_ref[n], preferred_element_type=F32)
        rs.append(ri[:, :bw])
        is_.append(ri[:, bw:])
    r = jax.nn.sigmoid(jnp.concatenate(rs, axis=1) + brg_ref[...])
    i_gate = jax.nn.sigmoid(jnp.concatenate(is_, axis=1) + big_ref[...])
    la = lrua_ref[...]
    softplus = jnp.maximum(la, 0.0) + jnp.log1p(jnp.exp(-jnp.abs(la)))
    a = jnp.exp(-LRU_C * r * softplus)
    u = jnp.sqrt(1.0 - a * a) * (i_gate * conv)
    nlc = d // LANES
    for c in range(nlc):
        abuf[c] = a[:, c * LANES:(c + 1) * LANES]
        hbuf[c] = u[:, c * LANES:(c + 1) * LANES]

    seg = ts // SUBLANES
    h_cols = []
    for c in range(nlc):
        def scan_step(j, carry, c=c):
            hl, prod = carry
            aj = abuf[c, pl.ds(j, SUBLANES, stride=seg), :]
            uj = hbuf[c, pl.ds(j, SUBLANES, stride=seg), :]
            hl = aj * hl + uj
            prod = aj * prod
            hbuf[c, pl.ds(j, SUBLANES, stride=seg), :] = hl
            pbuf[c, pl.ds(j, SUBLANES, stride=seg), :] = prod
            return hl, prod

        h_end, p_end = lax.fori_loop(0, seg, scan_step,
                                     (jnp.zeros((SUBLANES, LANES), F32), jnp.ones((SUBLANES, LANES), F32)),
                                     unroll=True)
        cstate = hcarry[:, c * LANES:(c + 1) * LANES]
        parts = []
        for rseg in range(SUBLANES):
            rows = slice(rseg * seg, (rseg + 1) * seg)
            parts.append(hbuf[c, rows, :] + pbuf[c, rows, :] * cstate)
            cstate = h_end[rseg:rseg + 1, :] + p_end[rseg:rseg + 1, :] * cstate
        hcarry[:, c * LANES:(c + 1) * LANES] = cstate
        h_cols.append(jnp.concatenate(parts, axis=0))
    h_seq = jnp.concatenate(h_cols, axis=1)

    ly = proj(o3 + d, o3 + 2 * d)
    gelu = 0.5 * ly * (1.0 + jnp.tanh(0.7978845608028654 * (ly + 0.044715 * (ly * ly * ly))))
    lru = h_seq * gelu
    ga = proj(o3 + 2 * d, o3 + 3 * d)
    merged = jax.nn.sigmoid(ga) * attnbuf[...]
    gl = proj(o3 + 3 * d, o3 + 4 * d)
    merged = merged + jax.nn.sigmoid(gl) * lru
    y = jnp.dot(merged.astype(BF16), wout_ref[...], preferred_element_type=F32) + bout_ref[...]
    o_ref[...] = x + gt1 * _rms(y, gpost_ref[...])


def _mixer(x, posb, mod, invf, g_pre, g_post, w_in, b_in, sinks, conv_w, conv_b, w_gate, b_rg, b_ig, lru_a,
           w_out, b_out):
    bsz, seq, d = x.shape
    ts = MIX_CHUNK
    assert seq % ts == 0 and ts % ATTN_BLOCK == 0
    in_w = w_in.shape[1]
    kvw = N_KV_HEADS * HEAD_DIM

    def whole(shape):
        return pl.BlockSpec(shape, lambda b, s: (0,) * len(shape))

    return pl.pallas_call(
        _mixer_kernel,
        grid=(bsz, seq // ts),
        in_specs=[
            pl.BlockSpec((None, ts, d), lambda b, s: (b, s, 0)),
            pl.BlockSpec((None, ts, LANES), lambda b, s: (b, s, 0)),
            pl.BlockSpec((None, 6, d), lambda b, s: (b, 0, 0)),
            whole((1, LANES)),
            whole((1, d)),
            whole((1, d)),
            whole((d, in_w)),
            whole((1, in_w)),
            pl.BlockSpec(memory_space=pltpu.SMEM),
            whole((CONV_WIDTH, d)),
            whole((1, d)),
            whole((LRU_BLOCKS, d // LRU_BLOCKS, 2 * d // LRU_BLOCKS)),
            whole((1, d)),
            whole((1, d)),
            whole((1, d)),
            whole((d, d)),
            whole((1, d)),
        ],
        out_specs=pl.BlockSpec((None, ts, d), lambda b, s: (b, s, 0)),
        out_shape=jax.ShapeDtypeStruct((bsz, seq, d), F32),
        scratch_shapes=[
            pltpu.VMEM((ATTN_BLOCK, kvw), F32),
            pltpu.VMEM((ATTN_BLOCK, kvw), F32),
            pltpu.VMEM((ts + SUBLANES, d), F32),
            pltpu.VMEM((1, d), F32),
            pltpu.VMEM((d // LANES, ts, LANES), F32),
            pltpu.VMEM((d // LANES, ts, LANES), F32),
            pltpu.VMEM((d // LANES, ts, LANES), F32),
            pltpu.VMEM((ts, d), F32),
        ],
        compiler_params=pltpu.CompilerParams(dimension_semantics=("arbitrary", "arbitrary"),
                                             vmem_limit_bytes=VMEM_LIMIT),
        name="mixer",
    )(x, posb, mod, invf, g_pre, g_post, w_in, b_in, sinks, conv_w, conv_b, w_gate, b_rg, b_ig, lru_a,
      w_out, b_out)


def _router_kernel(x_ref, mod_ref, g_ref, wr_hi_ref, wr_lo_ref, br_ref, h_ref, ti_ref, tw_ref):
    tr = x_ref.shape[0]
    sh2 = mod_ref[3:4, :]
    sc2 = mod_ref[4:5, :]
    h = _rms(x_ref[...], g_ref[...]) * (1.0 + sc2) + sh2
    h_ref[...] = h
    h_hi = h.astype(BF16)
    h_lo = (h - h_hi.astype(F32)).astype(BF16)
    logits = (jnp.dot(h_hi, wr_hi_ref[...], preferred_element_type=F32)
              + jnp.dot(h_hi, wr_lo_ref[...], preferred_element_type=F32)
              + jnp.dot(h_lo, wr_hi_ref[...], preferred_element_type=F32)) + br_ref[...]
    lane = lax.broadcasted_iota(jnp.int32, (tr, LANES), 1)
    work = logits
    vals, idxs = [], []
    for _ in range(TOP_K):
        m = jnp.max(work, axis=1, keepdims=True)
        idx = jnp.min(jnp.where(work == m, lane, LANES), axis=1, keepdims=True)
        vals.append(m)
        idxs.append(idx)
        work = jnp.where(lane == idx, -jnp.inf, work)
    es = [jnp.exp(vk - vals[0]) for vk in vals]
    tot = es[0] + es[1] + es[2] + es[3]
    tw = jnp.zeros((tr, LANES), F32)
    ti = jnp.zeros((tr, LANES), jnp.int32)
    for kk in range(TOP_K):
        tw = jnp.where(lane == kk, es[kk] / tot, tw)
        ti = jnp.where(lane == kk, idxs[kk], ti)
    tw_ref[...] = tw
    ti_ref[...] = ti


def _router(x1, mod, g, wr_hi, wr_lo, br):
    bsz, seq, d = x1.shape
    tr = ROUTE_TILE
    assert seq % tr == 0
    per = seq // tr
    t = bsz * seq
    x2 = x1.reshape(t, d)
    return pl.pallas_call(
        _router_kernel,
        grid=(t // tr,),
        in_specs=[
            pl.BlockSpec((tr, d), lambda i: (i, 0)),
            pl.BlockSpec((None, 6, d), lambda i: (i // per, 0, 0)),
            pl.BlockSpec((1, d), lambda i: (0, 0)),
            pl.BlockSpec((d, LANES), lambda i: (0, 0)),
            pl.BlockSpec((d, LANES), lambda i: (0, 0)),
            pl.BlockSpec((1, LANES), lambda i: (0, 0)),
        ],
        out_specs=[
            pl.BlockSpec((tr, d), lambda i: (i, 0)),
            pl.BlockSpec((tr, LANES), lambda i: (i, 0)),
            pl.BlockSpec((tr, LANES), lambda i: (i, 0)),
        ],
        out_shape=[
            jax.ShapeDtypeStruct((t, d), F32),
            jax.ShapeDtypeStruct((t, LANES), jnp.int32),
            jax.ShapeDtypeStruct((t, LANES), F32),
        ],
        compiler_params=pltpu.CompilerParams(dimension_semantics=("arbitrary",), vmem_limit_bytes=VMEM_LIMIT),
        name="router",
    )(x2, mod, g, wr_hi, wr_lo, br)


def _dispatch_kernel(pos_hbm, h_ref, xs_in, xs_ref, pos_smem, psem, sem):
    del xs_in
    i = pl.program_id(0)
    tm = h_ref.shape[0]
    pcopy = pltpu.make_async_copy(pos_hbm.at[i], pos_smem, psem)
    pcopy.start()
    pcopy.wait()

    def row_copy(t, p):
        return pltpu.make_async_copy(h_ref.at[pl.ds(t, 1)], xs_ref.at[pl.ds(p, 1)], sem)

    def issue(t, c):
        for kk in range(TOP_K):
            row_copy(t, pos_smem[TOP_K * t + kk]).start()
        return c

    lax.fori_loop(0, tm, issue, 0)

    def drain(t, c):
        for kk in range(TOP_K):
            row_copy(t, pos_smem[TOP_K * t + kk]).wait()
        return c

    lax.fori_loop(0, tm, drain, 0)


def _dispatch(pos, h2, n_slots):
    t, d = h2.shape
    tm = MOVE_TILE
    assert t % tm == 0
    pos2 = pos.reshape(t // tm, tm * TOP_K)
    xs0 = jnp.zeros((n_slots, d), h2.dtype)
    return pl.pallas_call(
        _dispatch_kernel,
        grid=(t // tm,),
        in_specs=[
            pl.BlockSpec(memory_space=pl.ANY),
            pl.BlockSpec((tm, d), lambda i: (i, 0)),
            pl.BlockSpec(memory_space=pl.ANY),
        ],
        out_specs=pl.BlockSpec(memory_space=pl.ANY),
        out_shape=jax.ShapeDtypeStruct((n_slots, d), h2.dtype),
        scratch_shapes=[
            pltpu.SMEM((tm * TOP_K,), jnp.int32),
            pltpu.SemaphoreType.DMA(()),
            pltpu.SemaphoreType.DMA(()),
        ],
        input_output_aliases={2: 0},
        compiler_params=pltpu.CompilerParams(dimension_semantics=("arbitrary",), vmem_limit_bytes=VMEM_LIMIT),
        name="dispatch",
    )(pos2, h2, xs0)


def _expert_kernel(be_ref, nu_ref, xs_ref, w1_ref, b1_ref, w2_ref, b2_ref, o_ref):
    i = pl.program_id(0)
    f = w2_ref.shape[0]

    @pl.when(i < nu_ref[0])
    def _():
        xb = xs_ref[...].astype(BF16)
        gu = jnp.dot(xb, w1_ref[...], preferred_element_type=F32) + b1_ref[...]
        g = jnp.minimum(gu[:, :f], SWIGLU_LIMIT)
        u = jnp.clip(gu[:, f:], -SWIGLU_LIMIT, SWIGLU_LIMIT)
        act = g * jax.nn.sigmoid(SWIGLU_ALPHA * g) * (u + 1.0)
        o_ref[...] = jnp.dot(act.astype(BF16), w2_ref[...], preferred_element_type=F32) + b2_ref[...]

    @pl.when(i >= nu_ref[0])
    def _():
        o_ref[...] = jnp.zeros_like(o_ref)


def _experts(blk_expert, n_used, xs, w1, b1, w2, b2):
    n_slots, d = xs.shape
    rows = EXPERT_ROWS
    n_blocks = n_slots // rows
    f2 = w1.shape[2]
    f = w2.shape[1]
    grid_spec = pltpu.PrefetchScalarGridSpec(
        num_scalar_prefetch=2,
        grid=(n_blocks,),
        in_specs=[
            pl.BlockSpec((rows, d), lambda i, be, nu: (i, 0)),
            pl.BlockSpec((None, d, f2), lambda i, be, nu: (be[i], 0, 0)),
            pl.BlockSpec((None, 1, f2), lambda i, be, nu: (be[i], 0, 0)),
            pl.BlockSpec((None, f, d), lambda i, be, nu: (be[i], 0, 0)),
            pl.BlockSpec((None, 1, d), lambda i, be, nu: (be[i], 0, 0)),
        ],
        out_specs=pl.BlockSpec((rows, d), lambda i, be, nu: (i, 0)),
    )
    return pl.pallas_call(
        _expert_kernel,
        grid_spec=grid_spec,
        out_shape=jax.ShapeDtypeStruct((n_slots, d), F32),
        compiler_params=pltpu.CompilerParams(dimension_semantics=("arbitrary",), vmem_limit_bytes=VMEM_LIMIT),
        name="experts",
    )(blk_expert, n_used, xs, w1, b1, w2, b2)


def _combine_kernel(pos_hbm, x_ref, mod_ref, g_ref, tw_ref, ys_hbm, o_ref, pos_smem, psem, buf, sem):
    i = pl.program_id(0)
    tc = x_ref.shape[0]
    pcopy = pltpu.make_async_copy(pos_hbm.at[i], pos_smem, psem)
    pcopy.start()
    pcopy.wait()

    def row_copy(t, kk, p):
        return pltpu.make_async_copy(ys_hbm.at[pl.ds(p, 1)], buf.at[kk, pl.ds(t, 1)], sem)

    def issue(t, c):
        for kk in range(TOP_K):
            row_copy(t, kk, pos_smem[TOP_K * t + kk]).start()
        return c

    lax.fori_loop(0, tc, issue, 0)

    def drain(t, c):
        for kk in range(TOP_K):
            row_copy(t, kk, pos_smem[TOP_K * t + kk]).wait()
        return c

    lax.fori_loop(0, tc, drain, 0)

    tw = tw_ref[...]
    y = buf[0] * tw[:, 0:1]
    for kk in range(1, TOP_K):
        y = y + buf[kk] * tw[:, kk:kk + 1]
    gt2 = mod_ref[5:6, :]
    o_ref[...] = x_ref[...] + gt2 * _rms(y, g_ref[...])


def _combine(pos, x1, mod, g, tw, ys):
    bsz, seq, d = x1.shape
    tc = COMB_TILE
    assert seq % tc == 0
    per = seq // tc
    t = bsz * seq
    pos2 = pos.reshape(t // tc, tc * TOP_K)
    out = pl.pallas_call(
        _combine_kernel,
        grid=(t // tc,),
        in_specs=[
            pl.BlockSpec(memory_space=pl.ANY),
            pl.BlockSpec((tc, d), lambda i: (i, 0)),
            pl.BlockSpec((None, 6, d), lambda i: (i // per, 0, 0)),
            pl.BlockSpec((1, d), lambda i: (0, 0)),
            pl.BlockSpec((tc, LANES), lambda i: (i, 0)),
            pl.BlockSpec(memory_space=pl.ANY),
        ],
        out_specs=pl.BlockSpec((tc, d), lambda i: (i, 0)),
        out_shape=jax.ShapeDtypeStruct((t, d), F32),
        scratch_shapes=[
            pltpu.SMEM((tc * TOP_K,), jnp.int32),
            pltpu.SemaphoreType.DMA(()),
            pltpu.VMEM((TOP_K, tc, d), F32),
            pltpu.SemaphoreType.DMA(()),
        ],
        compiler_params=pltpu.CompilerParams(dimension_semantics=("arbitrary",), vmem_limit_bytes=VMEM_LIMIT),
        name="combine",
    )(pos2, x1.reshape(t, d), mod, g, tw, ys)
    return out.reshape(bsz, seq, d)


def _slot_tables(top_i, n_blocks):
    rows = EXPERT_ROWS
    onehot = (top_i[:, :, None] == jnp.arange(N_EXPERTS, dtype=jnp.int32)[None, None, :]).astype(jnp.int32)
    multihot = jnp.sum(onehot, axis=1)
    incl = jnp.cumsum(multihot, axis=0)
    counts = incl[-1]
    pcounts = (counts + rows - 1) // rows * rows
    pends = jnp.cumsum(pcounts)
    base = (incl - multihot) + (pends - pcounts)[None, :]
    pos = jnp.sum(onehot * base[:, None, :], axis=-1).astype(jnp.int32)
    blk_expert = jnp.minimum(
        jnp.searchsorted(pends, jnp.arange(n_blocks, dtype=jnp.int32) * rows, side="right"), N_EXPERTS - 1
    ).astype(jnp.int32)
    n_used = (pends[-1] // rows).astype(jnp.int32).reshape(1)
    return pos, blk_expert, n_used


def kernel(x, c, positions, w_ada, b_ada, g_pre_mix, g_post_mix, w_in, b_in, sinks, conv_w, conv_b, w_rg, b_rg,
           w_ig, b_ig, lru_a, w_out, b_out, g_pre_ffn, g_post_ffn, w_router, b_router, w_e1, b_e1, w_e2, b_e2):
    bsz, seq, d = x.shape
    depth = w_ada.shape[0]
    t = bsz * seq
    inv_freq = ROPE_THETA ** (-jnp.arange(0, HEAD_DIM, 2, dtype=F32) / HEAD_DIM)
    invf = jnp.tile(inv_freq, LANES // (HEAD_DIM // 2)).reshape(1, LANES)
    posb = jnp.broadcast_to(positions.astype(F32)[:, :, None], (bsz, seq, LANES))
    n_blocks = (t * TOP_K) // EXPERT_ROWS + N_EXPERTS
    n_slots = n_blocks * EXPERT_ROWS
    row = lambda a: a.reshape(1, -1)
    for l in range(depth):
        mod = _ada(c, w_ada[l], b_ada[l]).reshape(bsz, 6, d)
        w_gate = jnp.concatenate([w_rg[l], w_ig[l]], axis=-1).astype(BF16)
        x = _mixer(x, posb, mod, invf, row(g_pre_mix[l]), row(g_post_mix[l]), w_in[l].astype(BF16), row(b_in[l]),
                   row(sinks[l]), conv_w[l], row(conv_b[l]), w_gate, row(b_rg[l]), row(b_ig[l]), row(lru_a[l]),
                   w_out[l].astype(BF16), row(b_out[l]))
        wr = jnp.pad(w_router[l].astype(F32), ((0, 0), (0, LANES - N_EXPERTS)))
        wr_hi = wr.astype(BF16)
        wr_lo = (wr - wr_hi.astype(F32)).astype(BF16)
        br = jnp.pad(b_router[l].astype(F32), (0, LANES - N_EXPERTS), constant_values=NEG_BIG).reshape(1, LANES)
        h2, ti, tw = _router(x, mod, row(g_pre_ffn[l]), wr_hi, wr_lo, br)
        pos, blk_expert, n_used = _slot_tables(ti[:, :TOP_K], n_blocks)
        xs = _dispatch(pos, h2, n_slots)
        ys = _experts(blk_expert, n_used, xs, w_e1[l].astype(BF16), b_e1[l][:, None, :], w_e2[l].astype(BF16),
                      b_e2[l][:, None, :])
        x = _combine(pos, x, mod, row(g_post_ffn[l]), tw, ys)
    return x
```

```python
import functools

import jax
import jax.numpy as jnp
from jax import lax
from jax.experimental import pallas as pl
from jax.experimental.pallas import tpu as pltpu

F32 = jnp.float32
BF16 = jnp.bfloat16

N_Q_HEADS = 16
N_KV_HEADS = 4
HEAD_DIM = 64
ATTN_BLOCK = 128
ROPE_THETA = 10000.0
LRU_BLOCKS = 8
CONV_WIDTH = 4
LRU_C = 8.0
N_EXPERTS = 32
TOP_K = 4
SWIGLU_LIMIT = 7.0
SWIGLU_ALPHA = 1.702
EPS = 1e-6

LANES = 128
SUBLANES = 8
NEG_BIG = -1e30

MIX_CHUNK = 256
ROUTE_TILE = 512
MOVE_TILE = 512
COMB_TILE = 256
EXPERT_ROWS = 256
VMEM_LIMIT = 56 * 1024 * 1024


def _rms(x, g):
    ms = jnp.mean(x * x, axis=-1, keepdims=True)
    return x * lax.rsqrt(ms + EPS) * g


def _ada_kernel(c_ref, w_ref, b_ref, o_ref):
    c = c_ref[...]
    a = c * jax.nn.sigmoid(c)
    o_ref[...] = jnp.dot(a, w_ref[...], preferred_element_type=F32) + b_ref[...]


def _ada(c, w, b):
    bsz, d = c.shape
    n = w.shape[1] // d
    return pl.pallas_call(
        _ada_kernel,
        grid=(n,),
        in_specs=[
            pl.BlockSpec((bsz, d), lambda j: (0, 0)),
            pl.BlockSpec((d, d), lambda j: (0, j)),
            pl.BlockSpec((1, d), lambda j: (0, j)),
        ],
        out_specs=pl.BlockSpec((bsz, d), lambda j: (0, j)),
        out_shape=jax.ShapeDtypeStruct((bsz, w.shape[1]), F32),
        compiler_params=pltpu.CompilerParams(dimension_semantics=("arbitrary",), vmem_limit_bytes=VMEM_LIMIT),
        name="ada",
    )(c, w, b.reshape(1, -1))


def _mixer_kernel(x_ref, pos_ref, mod_ref, invf_ref, gpre_ref, gpost_ref, win_ref, bin_ref, sinks_ref,
                  convw_ref, convb_ref, wgate_ref, brg_ref, big_ref, lrua_ref, wout_ref, bout_ref,
                  o_ref, kprev, vprev, lxbuf, hcarry, abuf, hbuf, pbuf, attnbuf):
    ts, d = x_ref.shape
    qw = N_Q_HEADS * HEAD_DIM
    kvw = N_KV_HEADS * HEAD_DIM
    s_idx = pl.program_id(1)

    @pl.when(s_idx == 0)
    def _():
        kprev[...] = jnp.zeros_like(kprev)
        vprev[...] = jnp.zeros_like(vprev)
        lxbuf[0:SUBLANES, :] = jnp.zeros((SUBLANES, d), F32)
        hcarry[...] = jnp.zeros_like(hcarry)

    x = x_ref[...]
    sh1 = mod_ref[0:1, :]
    sc1 = mod_ref[1:2, :]
    gt1 = mod_ref[2:3, :]
    h = _rms(x, gpre_ref[...]) * (1.0 + sc1) + sh1
    hb = h.astype(BF16)

    def proj(lo, hi):
        return jnp.dot(hb, win_ref[:, lo:hi], preferred_element_type=F32) + bin_ref[:, lo:hi]

    ang = pos_ref[...] * invf_ref[...]
    cos = jnp.cos(ang)
    sin = jnp.sin(ang)
    lane = lax.broadcasted_iota(jnp.int32, (ts, LANES), 1)
    first_half = (lane & (HEAD_DIM // 2)) == 0
    sin_signed = jnp.where(first_half, -sin, sin)

    def rope(t):
        rot = jnp.where(first_half, pltpu.roll(t, LANES - HEAD_DIM // 2, 1), pltpu.roll(t, HEAD_DIM // 2, 1))
        return t * cos + rot * sin_signed

    q = proj(0, qw)
    scale = HEAD_DIM ** -0.5
    qb = jnp.concatenate([rope(q[:, j * LANES:(j + 1) * LANES]) * scale for j in range(qw // LANES)],
                         axis=1).astype(BF16)
    k = proj(qw, qw + kvw)
    kr = jnp.concatenate([rope(k[:, j * LANES:(j + 1) * LANES]) for j in range(kvw // LANES)], axis=1)
    v = proj(qw + kvw, qw + 2 * kvw)
    kext = jnp.concatenate([kprev[...], kr], axis=0)
    vext = jnp.concatenate([vprev[...], v], axis=0)
    kprev[...] = kr[ts - ATTN_BLOCK:, :]
    vprev[...] = v[ts - ATTN_BLOCK:, :]

    lane_e = lax.broadcasted_iota(jnp.int32, (ts + ATTN_BLOCK, LANES), 1)
    low = lane_e < HEAD_DIM

    def head_variants(t):
        lo_ = jnp.where(low, t, 0.0)
        hi_ = jnp.where(low, 0.0, t)
        lo_sw = pltpu.roll(lo_, HEAD_DIM, 1)
        hi_sw = pltpu.roll(hi_, HEAD_DIM, 1)
        return ((lo_.astype(BF16), lo_sw.astype(BF16)), (hi_sw.astype(BF16), hi_.astype(BF16)))

    kvar = []
    vvar = []
    for c in range(kvw // LANES):
        ke, ko = head_variants(kext[:, c * LANES:(c + 1) * LANES])
        ve, vo = head_variants(vext[:, c * LANES:(c + 1) * LANES])
        kvar += [ke, ko]
        vvar += [ve, vo]

    rows2 = 2 * ATTN_BLOCK
    qi = lax.broadcasted_iota(jnp.int32, (rows2, rows2), 0) & (ATTN_BLOCK - 1)
    kj = lax.broadcasted_iota(jnp.int32, (rows2, rows2), 1)
    cur_ok = (kj >= ATTN_BLOCK) & ((kj - ATTN_BLOCK) <= qi)
    prev_ok = (kj < ATTN_BLOCK) & (kj > qi)
    bias_mid = jnp.where(cur_ok | prev_ok, 0.0, NEG_BIG)
    bias_first = jnp.where(cur_ok | (prev_ok & (s_idx > 0)), 0.0, NEG_BIG)
    upper_rows = lax.broadcasted_iota(jnp.int32, (rows2, 1), 0) < ATTN_BLOCK

    group = N_Q_HEADS // N_KV_HEADS
    for blk in range(ts // ATTN_BLOCK):
        r0 = blk * ATTN_BLOCK
        bias = bias_first if blk == 0 else bias_mid
        for g in range(N_KV_HEADS):
            (k_a, k_b), (v_a, v_b) = kvar[g], vvar[g]
            qg = jnp.concatenate([qb[r0:r0 + ATTN_BLOCK, (2 * g) * LANES:(2 * g + 1) * LANES],
                                  qb[r0:r0 + ATTN_BLOCK, (2 * g + 1) * LANES:(2 * g + 2) * LANES]], axis=0)
            og = None
            for half, (kk, vv) in enumerate(((k_a, v_a), (k_b, v_b))):
                sc = lax.dot_general(qg, kk[r0:r0 + rows2, :], (((1,), (1,)), ((), ())),
                                     preferred_element_type=F32) + bias
                sink = jnp.where(upper_rows, sinks_ref[0, group * g + half], sinks_ref[0, group * g + 2 + half])
                m = jnp.maximum(jnp.max(sc, axis=1, keepdims=True), sink)
                p = jnp.exp(sc - m)
                denom = jnp.sum(p, axis=1, keepdims=True) + jnp.exp(sink - m)
                o = jnp.dot(p.astype(BF16), vv[r0:r0 + rows2, :], preferred_element_type=F32) / denom
                og = o if og is None else og + o
            attnbuf[r0:r0 + ATTN_BLOCK, (2 * g) * LANES:(2 * g + 1) * LANES] = og[:ATTN_BLOCK, :]
            attnbuf[r0:r0 + ATTN_BLOCK, (2 * g + 1) * LANES:(2 * g + 2) * LANES] = og[ATTN_BLOCK:, :]

    o3 = qw + 2 * kvw
    lxbuf[SUBLANES:SUBLANES + ts, :] = proj(o3, o3 + d)
    conv = convb_ref[...]
    for tap in range(CONV_WIDTH):
        off = SUBLANES - (CONV_WIDTH - 1) + tap
        conv = conv + lxbuf[off:off + ts, :] * convw_ref[tap:tap + 1, :]
    lxbuf[0:SUBLANES, :] = lxbuf[ts:ts + SUBLANES, :]
    cb = conv.astype(BF16)
    bw = d // LRU_BLOCKS
    rs, is_ = [], []
    for n in range(LRU_BLOCKS):
        ri = jnp.dot(cb[:, n * bw:(n + 1) * bw], wgate_ref[n], preferred_element_type=F32)
        rs.append(ri[:, :bw])
        is_.append(ri[:, bw:])
    r = jax.nn.sigmoid(jnp.concatenate(rs, axis=1) + brg_ref[...])
    i_gate = jax.nn.sigmoid(jnp.concatenate(is_, axis=1) + big_ref[...])
    la = lrua_ref[...]
    softplus = jnp.maximum(la, 0.0) + jnp.log1p(jnp.exp(-jnp.abs(la)))
    a = jnp.exp(-LRU_C * r * softplus)
    u = jnp.sqrt(1.0 - a * a) * (i_gate * conv)
    nlc = d // LANES
    for c in range(nlc):
        abuf[c] = a[:, c * LANES:(c + 1) * LANES]
        hbuf[c] = u[:, c * LANES:(c + 1) * LANES]

    seg = ts // SUBLANES
    h_cols = []
    for c in range(nlc):
        def scan_step(j, carry, c=c):
            hl, prod = carry
            aj = abuf[c, pl.ds(j, SUBLANES, stride=seg), :]
            uj = hbuf[c, pl.ds(j, SUBLANES, stride=seg), :]
            hl = aj * hl + uj
            prod = aj * prod
            hbuf[c, pl.ds(j, SUBLANES, stride=seg), :] = hl
            pbuf[c, pl.ds(j, SUBLANES, stride=seg), :] = prod
            return hl, prod

        h_end, p_end = lax.fori_loop(0, seg, scan_step,
                                     (jnp.zeros((SUBLANES, LANES), F32), jnp.ones((SUBLANES, LANES), F32)),
                                     unroll=True)
        cstate = hcarry[:, c * LANES:(c + 1) * LANES]
        parts = []
        for rseg in range(SUBLANES):
            rows = slice(rseg * seg, (rseg + 1) * seg)
            parts.append(hbuf[c, rows, :] + pbuf[c, rows, :] * cstate)
            cstate = h_end[rseg:rseg + 1, :] + p_end[rseg:rseg + 1, :] * cstate
        hcarry[:, c * LANES:(c + 1) * LANES] = cstate
        h_cols.append(jnp.concatenate(parts, axis=0))
    h_seq = jnp.concatenate(h_cols, axis=1)

    ly = proj(o3 + d, o3 + 2 * d)
    gelu = 0.5 * ly * (1.0 + jnp.tanh(0.7978845608028654 * (ly + 0.044715 * (ly * ly * ly))))
    lru = h_seq * gelu
    ga = proj(o3 + 2 * d, o3 + 3 * d)
    merged = jax.nn.sigmoid(ga) * attnbuf[...]
    gl = proj(o3 + 3 * d, o3 + 4 * d)
    merged = merged + jax.nn.sigmoid(gl) * lru
    y = jnp.dot(merged.astype(BF16), wout_ref[...], preferred_element_type=F32) + bout_ref[...]
    o_ref[...] = x + gt1 * _rms(y, gpost_ref[...])


def _mixer(x, posb, mod, invf, g_pre, g_post, w_in, b_in, sinks, conv_w, conv_b, w_gate, b_rg, b_ig, lru_a,
           w_out, b_out):
    bsz, seq, d = x.shape
    ts = MIX_CHUNK
    assert seq % ts == 0 and ts % ATTN_BLOCK == 0
    in_w = w_in.shape[1]
    kvw = N_KV_HEADS * HEAD_DIM

    def whole(shape):
        return pl.BlockSpec(shape, lambda b, s: (0,) * len(shape))

    return pl.pallas_call(
        _mixer_kernel,
        grid=(bsz, seq // ts),
        in_specs=[
            pl.BlockSpec((None, ts, d), lambda b, s: (b, s, 0)),
            pl.BlockSpec((None, ts, LANES), lambda b, s: (b, s, 0)),
            pl.BlockSpec((None, 6, d), lambda b, s: (b, 0, 0)),
            whole((1, LANES)),
            whole((1, d)),
            whole((1, d)),
            whole((d, in_w)),
            whole((1, in_w)),
            pl.BlockSpec(memory_space=pltpu.SMEM),
            whole((CONV_WIDTH, d)),
            whole((1, d)),
            whole((LRU_BLOCKS, d // LRU_BLOCKS, 2 * d // LRU_BLOCKS)),
            whole((1, d)),
            whole((1, d)),
            whole((1, d)),
            whole((d, d)),
            whole((1, d)),
        ],
        out_specs=pl.BlockSpec((None, ts, d), lambda b, s: (b, s, 0)),
        out_shape=jax.ShapeDtypeStruct((bsz, seq, d), F32),
        scratch_shapes=[
            pltpu.VMEM((ATTN_BLOCK, kvw), F32),
            pltpu.VMEM((ATTN_BLOCK, kvw), F32),
            pltpu.VMEM((ts + SUBLANES, d), F32),
            pltpu.VMEM((1, d), F32),
            pltpu.VMEM((d // LANES, ts, LANES), F32),
            pltpu.VMEM((d // LANES, ts, LANES), F32),
            pltpu.VMEM((d // LANES, ts, LANES), F32),
            pltpu.VMEM((ts, d), F32),
        ],
        compiler_params=pltpu.CompilerParams(dimension_semantics=("arbitrary", "arbitrary"),
                                             vmem_limit_bytes=VMEM_LIMIT),
        name="mixer",
    )(x, posb, mod, invf, g_pre, g_post, w_in, b_in, sinks, conv_w, conv_b, w_gate, b_rg, b_ig, lru_a,
      w_out, b_out)


def _router_kernel(x_ref, mod_ref, g_ref, wr_hi_ref, wr_lo_ref, br_ref, h_ref, ti_ref, tw_ref):
    tr = x_ref.shape[0]
    sh2 = mod_ref[3:4, :]
    sc2 = mod_ref[4:5, :]
    h = _rms(x_ref[...], g_ref[...]) * (1.0 + sc2) + sh2
    for j in range(h.shape[1] // LANES):
        h_ref[pl.ds(j, tr, stride=SUBLANES), :] = h[:, j * LANES:(j + 1) * LANES]
    h_hi = h.astype(BF16)
    h_lo = (h - h_hi.astype(F32)).astype(BF16)
    logits = (jnp.dot(h_hi, wr_hi_ref[...], preferred_element_type=F32)
              + jnp.dot(h_hi, wr_lo_ref[...], preferred_element_type=F32)
              + jnp.dot(h_lo, wr_hi_ref[...], preferred_element_type=F32)) + br_ref[...]
    lane = lax.broadcasted_iota(jnp.int32, (tr, LANES), 1)
    work = logits
    vals, idxs = [], []
    for _ in range(TOP_K):
        m = jnp.max(work, axis=1, keepdims=True)
        idx = jnp.min(jnp.where(work == m, lane, LANES), axis=1, keepdims=True)
        vals.append(m)
        idxs.append(idx)
        work = jnp.where(lane == idx, -jnp.inf, work)
    es = [jnp.exp(vk - vals[0]) for vk in vals]
    tot = es[0] + es[1] + es[2] + es[3]
    tw = jnp.zeros((tr, LANES), F32)
    ti = jnp.zeros((tr, LANES), jnp.int32)
    for kk in range(TOP_K):
        tw = jnp.where(lane == kk, es[kk] / tot, tw)
        ti = jnp.where(lane == kk, idxs[kk], ti)
    tw_ref[...] = tw
    ti_ref[...] = ti


def _router(x1, mod, g, wr_hi, wr_lo, br):
    bsz, seq, d = x1.shape
    tr = ROUTE_TILE
    assert seq % tr == 0
    per = seq // tr
    t = bsz * seq
    x2 = x1.reshape(t, d)
    return pl.pallas_call(
        _router_kernel,
        grid=(t // tr,),
        in_specs=[
            pl.BlockSpec((tr, d), lambda i: (i, 0)),
            pl.BlockSpec((None, 6, d), lambda i: (i // per, 0, 0)),
            pl.BlockSpec((1, d), lambda i: (0, 0)),
            pl.BlockSpec((d, LANES), lambda i: (0, 0)),
            pl.BlockSpec((d, LANES), lambda i: (0, 0)),
            pl.BlockSpec((1, LANES), lambda i: (0, 0)),
        ],
        out_specs=[
            pl.BlockSpec((tr * d // LANES, LANES), lambda i: (i, 0)),
            pl.BlockSpec((tr, LANES), lambda i: (i, 0)),
            pl.BlockSpec((tr, LANES), lambda i: (i, 0)),
        ],
        out_shape=[
            jax.ShapeDtypeStruct((t * d // LANES, LANES), F32),
            jax.ShapeDtypeStruct((t, LANES), jnp.int32),
            jax.ShapeDtypeStruct((t, LANES), F32),
        ],
        compiler_params=pltpu.CompilerParams(dimension_semantics=("arbitrary",), vmem_limit_bytes=VMEM_LIMIT),
        name="router",
    )(x2, mod, g, wr_hi, wr_lo, br)


ROW_COPY_UNROLL = 8


def _expert_kernel(be_ref, idx_hbm, src0_hbm, h_hbm, w1a, b1a, w2a, b2a, w1b, b1b, w2b, b2b, y_hbm,
                   idx_smem, isem, xa, xb, oa, ob, gsem, ssem):
    del be_ref
    s = pl.program_id(0)
    ns = pl.num_programs(0)
    tile = SUBLANES
    rows = xa.shape[0] // tile
    f = w2a.shape[0]
    nlc = w2a.shape[1] // LANES
    slot = s % 2
    trash0 = y_hbm.shape[0] - 2 * rows * tile

    def idx_copy(j):
        return pltpu.make_async_copy(idx_hbm.at[j], idx_smem.at[pl.ds((j % 2) * 4 * rows, 4 * rows)],
                                     isem.at[j % 2])

    def token_tile(ref, t):
        return ref.at[pl.ds(pl.multiple_of(t * tile, tile), tile)]

    def gather_rows(xref, sem_i, islot, base):
        off = islot * (4 * rows) + base

        def body(r, c):
            pltpu.make_async_copy(token_tile(h_hbm, idx_smem[off + r]), token_tile(xref, r),
                                  gsem.at[sem_i]).start()
            return c

        lax.fori_loop(0, rows, body, 0, unroll=ROW_COPY_UNROLL)

    def scatter_rows(oref, sem_i, base):
        off = slot * (4 * rows) + base

        def body(r, c):
            pltpu.make_async_copy(token_tile(oref, r), token_tile(y_hbm, idx_smem[off + r]),
                                  ssem.at[sem_i]).start()
            return c

        lax.fori_loop(0, rows, body, 0, unroll=ROW_COPY_UNROLL)

    def wait_gather(xref, sem_i):
        pltpu.make_async_copy(h_hbm.at[pl.ds(0, rows * tile)], xref, gsem.at[sem_i]).wait()

    def wait_scatter(oref, sem_i):
        pltpu.make_async_copy(oref, y_hbm.at[pl.ds(0, rows * tile)], ssem.at[sem_i]).wait()

    @pl.when(s == 0)
    def _():
        oa[...] = jnp.zeros_like(oa)
        ob[...] = jnp.zeros_like(ob)
        stage = pltpu.make_async_copy(src0_hbm, idx_smem.at[pl.ds(4 * rows, rows)], isem.at[1])
        stage.start()
        stage.wait()
        gather_rows(xa, 0, 1, 0)
        idx_copy(0).start()
        pltpu.make_async_copy(oa, y_hbm.at[pl.ds(trash0, rows * tile)], ssem.at[0]).start()

    idx_copy(s).wait()

    @pl.when(s + 1 < ns)
    def _():
        idx_copy(s + 1).start()

    def phase(xref, oref, w1, b1, w2, b2, move):
        xv = jnp.concatenate([xref[pl.ds(j, rows, stride=tile), :] for j in range(nlc)], axis=1).astype(BF16)
        gu = jnp.dot(xv, w1[...], preferred_element_type=F32) + b1[...]
        move()
        g = jnp.minimum(gu[:, :f], SWIGLU_LIMIT)
        u = jnp.clip(gu[:, f:], -SWIGLU_LIMIT, SWIGLU_LIMIT)
        act = (g * jax.nn.sigmoid(SWIGLU_ALPHA * g) * (u + 1.0)).astype(BF16)
        out = jnp.dot(act, w2[...], preferred_element_type=F32) + b2[...]
        for j in range(nlc):
            oref[pl.ds(j, rows, stride=tile), :] = out[:, j * LANES:(j + 1) * LANES]

    def move_a():
        gather_rows(xb, 1, slot, 0)
        scatter_rows(ob, 1, rows)

    def move_b():
        gather_rows(xa, 0, slot, 2 * rows)
        scatter_rows(oa, 0, 3 * rows)

    wait_gather(xa, 0)
    wait_scatter(oa, 0)
    phase(xa, oa, w1a, b1a, w2a, b2a, move_a)
    wait_gather(xb, 1)
    wait_scatter(ob, 1)
    phase(xb, ob, w1b, b1b, w2b, b2b, move_b)

    @pl.when(s == ns - 1)
    def _():
        wait_gather(xa, 0)
        wait_scatter(oa, 0)


def _experts(blk_expert, idx, src0, h2, w1, b1, w2, b2, n_out_rows):
    rows = EXPERT_ROWS
    n_steps = idx.shape[0]
    d, f2 = w1.shape[1], w1.shape[2]
    f = w2.shape[1]
    assert d == SUBLANES * LANES and h2.shape[1] == LANES
    tile_rows = rows * SUBLANES

    def wspecs(which):
        return [
            pl.BlockSpec((None, d, f2), lambda i, be: (be[2 * i + which], 0, 0)),
            pl.BlockSpec((None, 1, f2), lambda i, be: (be[2 * i + which], 0, 0)),
            pl.BlockSpec((None, f, d), lambda i, be: (be[2 * i + which], 0, 0)),
            pl.BlockSpec((None, 1, d), lambda i, be: (be[2 * i + which], 0, 0)),
        ]

    grid_spec = pltpu.PrefetchScalarGridSpec(
        num_scalar_prefetch=1,
        grid=(n_steps,),
        in_specs=[pl.BlockSpec(memory_space=pl.ANY)] * 3 + wspecs(0) + wspecs(1),
        out_specs=pl.BlockSpec(memory_space=pl.ANY),
        scratch_shapes=[
            pltpu.SMEM((2 * 4 * rows,), jnp.int32),
            pltpu.SemaphoreType.DMA((2,)),
            pltpu.VMEM((tile_rows, LANES), F32),
            pltpu.VMEM((tile_rows, LANES), F32),
            pltpu.VMEM((tile_rows, LANES), F32),
            pltpu.VMEM((tile_rows, LANES), F32),
            pltpu.SemaphoreType.DMA((2,)),
            pltpu.SemaphoreType.DMA((2,)),
        ],
    )
    return pl.pallas_call(
        _expert_kernel,
        grid_spec=grid_spec,
        out_shape=jax.ShapeDtypeStruct((n_out_rows * SUBLANES, LANES), F32),
        compiler_params=pltpu.CompilerParams(dimension_semantics=("arbitrary",), vmem_limit_bytes=VMEM_LIMIT),
        name="experts",
    )(blk_expert, idx, src0, h2, w1, b1, w2, b2, w1, b1, w2, b2)


def _combine_kernel(x_ref, mod_ref, g_ref, tw_ref, y0_ref, y1_ref, y2_ref, y3_ref, o_ref):
    tc, d = x_ref.shape
    tw = tw_ref[...]
    cols = []
    for j in range(d // LANES):
        acc = None
        for kk, yk in enumerate((y0_ref, y1_ref, y2_ref, y3_ref)):
            term = yk[pl.ds(j, tc, stride=SUBLANES), :] * tw[:, kk:kk + 1]
            acc = term if acc is None else acc + term
        cols.append(acc)
    y = jnp.concatenate(cols, axis=1)
    gt2 = mod_ref[5:6, :]
    o_ref[...] = x_ref[...] + gt2 * _rms(y, g_ref[...])


def _combine(x1, mod, g, tw, y4):
    bsz, seq, d = x1.shape
    tc = COMB_TILE
    assert seq % tc == 0
    per = seq // tc
    t = bsz * seq
    nt = t // tc

    def yspec(kk):
        return pl.BlockSpec((tc * SUBLANES, LANES), lambda i: (kk * nt + i, 0))

    out = pl.pallas_call(
        _combine_kernel,
        grid=(nt,),
        in_specs=[
            pl.BlockSpec((tc, d), lambda i: (i, 0)),
            pl.BlockSpec((None, 6, d), lambda i: (i // per, 0, 0)),
            pl.BlockSpec((1, d), lambda i: (0, 0)),
            pl.BlockSpec((tc, LANES), lambda i: (i, 0)),
            yspec(0), yspec(1), yspec(2), yspec(3),
        ],
        out_specs=pl.BlockSpec((tc, d), lambda i: (i, 0)),
        out_shape=jax.ShapeDtypeStruct((t, d), F32),
        compiler_params=pltpu.CompilerParams(dimension_semantics=("arbitrary",), vmem_limit_bytes=VMEM_LIMIT),
        name="combine",
    )(x1.reshape(t, d), mod, g, tw, y4, y4, y4, y4)
    return out.reshape(bsz, seq, d)


def _slot_tables(top_i, n_blocks):
    rows = EXPERT_ROWS
    onehot = (top_i[:, :, None] == jnp.arange(N_EXPERTS, dtype=jnp.int32)[None, None, :]).astype(jnp.int32)
    multihot = jnp.sum(onehot, axis=1)
    incl = jnp.cumsum(multihot, axis=0)
    counts = incl[-1]
    pcounts = (counts + rows - 1) // rows * rows
    pends = jnp.cumsum(pcounts)
    base = (incl - multihot) + (pends - pcounts)[None, :]
    pos = jnp.sum(onehot * base[:, None, :], axis=-1).astype(jnp.int32)
    blk_expert = jnp.minimum(
        jnp.searchsorted(pends, jnp.arange(n_blocks + 2, dtype=jnp.int32) * rows, side="right"), N_EXPERTS - 1
    ).astype(jnp.int32)
    t = top_i.shape[0]
    front, back = 1, 3
    n_ext = n_blocks + front + back
    assign = jnp.full((n_ext * rows,), -1, jnp.int32).at[pos.reshape(-1) + front * rows].set(
        jnp.arange(t * TOP_K, dtype=jnp.int32), unique_indices=True)
    slot = jnp.arange(n_ext * rows, dtype=jnp.int32) + (2 - front) * rows
    tok = assign // TOP_K
    src = jnp.where(assign >= 0, tok, 0).reshape(n_ext, rows)
    dst = jnp.where(assign >= 0, (assign % TOP_K) * t + tok, TOP_K * t + (slot % (2 * rows))).reshape(n_ext, rows)
    n_steps = n_blocks // 2 + 1
    idx = jnp.concatenate([src[2::2][:n_steps], dst[0::2][:n_steps], src[3::2][:n_steps], dst[1::2][:n_steps]],
                          axis=1)
    return idx, src[front], blk_expert


def kernel(x, c, positions, w_ada, b_ada, g_pre_mix, g_post_mix, w_in, b_in, sinks, conv_w, conv_b, w_rg, b_rg,
           w_ig, b_ig, lru_a, w_out, b_out, g_pre_ffn, g_post_ffn, w_router, b_router, w_e1, b_e1, w_e2, b_e2):
    bsz, seq, d = x.shape
    depth = w_ada.shape[0]
    t = bsz * seq
    inv_freq = ROPE_THETA ** (-jnp.arange(0, HEAD_DIM, 2, dtype=F32) / HEAD_DIM)
    invf = jnp.tile(inv_freq, LANES // (HEAD_DIM // 2)).reshape(1, LANES)
    posb = jnp.broadcast_to(positions.astype(F32)[:, :, None], (bsz, seq, LANES))
    n_blocks = (t * TOP_K) // EXPERT_ROWS + N_EXPERTS
    n_slots = n_blocks * EXPERT_ROWS
    row = lambda a: a.reshape(1, -1)
    for l in range(depth):
        mod = _ada(c, w_ada[l], b_ada[l]).reshape(bsz, 6, d)
        w_gate = jnp.concatenate([w_rg[l], w_ig[l]], axis=-1).astype(BF16)
        x = _mixer(x, posb, mod, invf, row(g_pre_mix[l]), row(g_post_mix[l]), w_in[l].astype(BF16), row(b_in[l]),
                   row(sinks[l]), conv_w[l], row(conv_b[l]), w_gate, row(b_rg[l]), row(b_ig[l]), row(lru_a[l]),
                   w_out[l].astype(BF16), row(b_out[l]))
        wr = jnp.pad(w_router[l].astype(F32), ((0, 0), (0, LANES - N_EXPERTS)))
        wr_hi = wr.astype(BF16)
        wr_lo = (wr - wr_hi.astype(F32)).astype(BF16)
        br = jnp.pad(b_router[l].astype(F32), (0, LANES - N_EXPERTS), constant_values=NEG_BIG).reshape(1, LANES)
        h2, ti, tw = _router(x, mod, row(g_pre_ffn[l]), wr_hi, wr_lo, br)
        idx, src0, blk_expert = _slot_tables(ti[:, :TOP_K], n_blocks)
        y4 = _experts(blk_expert, idx, src0, h2, w_e1[l].astype(BF16), b_e1[l][:, None, :],
                      w_e2[l].astype(BF16), b_e2[l][:, None, :], TOP_K * t + 2 * EXPERT_ROWS)
        x = _combine(x, mod, row(g_post_ffn[l]), tw, y4)
    return x
```

```python
import functools

import jax
import jax.numpy as jnp
from jax import lax
from jax.experimental import pallas as pl
from jax.experimental.pallas import tpu as pltpu

F32 = jnp.float32
BF16 = jnp.bfloat16

N_Q_HEADS = 16
N_KV_HEADS = 4
HEAD_DIM = 64
ATTN_BLOCK = 128
ROPE_THETA = 10000.0
LRU_BLOCKS = 8
CONV_WIDTH = 4
LRU_C = 8.0
N_EXPERTS = 32
TOP_K = 4
SWIGLU_LIMIT = 7.0
SWIGLU_ALPHA = 1.702
EPS = 1e-6

LANES = 128
SUBLANES = 8
NEG_BIG = -1e30

MIX_CHUNK = 256
ROUTE_TILE = 512
DISPATCH_TILE = 512
COMB_TILE = 256
EXPERT_ROWS = 256
VMEM_LIMIT = 56 * 1024 * 1024


def _rms(x, g):
    ms = jnp.mean(x * x, axis=-1, keepdims=True)
    return x * lax.rsqrt(ms + EPS) * g


def _ada_kernel(c_ref, w_ref, b_ref, o_ref):
    c = c_ref[...]
    a = c * jax.nn.sigmoid(c)
    o_ref[...] = jnp.dot(a, w_ref[...], preferred_element_type=F32) + b_ref[...]


def _ada(c, w, b):
    bsz, d = c.shape
    n = w.shape[1] // d
    return pl.pallas_call(
        _ada_kernel,
        grid=(n,),
        in_specs=[
            pl.BlockSpec((bsz, d), lambda j: (0, 0)),
            pl.BlockSpec((d, d), lambda j: (0, j)),
            pl.BlockSpec((1, d), lambda j: (0, j)),
        ],
        out_specs=pl.BlockSpec((bsz, d), lambda j: (0, j)),
        out_shape=jax.ShapeDtypeStruct((bsz, w.shape[1]), F32),
        compiler_params=pltpu.CompilerParams(dimension_semantics=("arbitrary",), vmem_limit_bytes=VMEM_LIMIT),
        name="ada",
    )(c, w, b.reshape(1, -1))


def _mixer_kernel(x_ref, pos_ref, mod_ref, invf_ref, gpre_ref, gpost_ref, win_ref, bin_ref, sinks_ref,
                  convw_ref, convb_ref, wgate_ref, brg_ref, big_ref, lrua_ref, wout_ref, bout_ref,
                  o_ref, kprev, vprev, lxbuf, hcarry, abuf, hbuf, pbuf, attnbuf):
    ts, d = x_ref.shape
    qw = N_Q_HEADS * HEAD_DIM
    kvw = N_KV_HEADS * HEAD_DIM
    s_idx = pl.program_id(1)

    @pl.when(s_idx == 0)
    def _():
        kprev[...] = jnp.zeros_like(kprev)
        vprev[...] = jnp.zeros_like(vprev)
        lxbuf[0:SUBLANES, :] = jnp.zeros((SUBLANES, d), F32)
        hcarry[...] = jnp.zeros_like(hcarry)

    x = x_ref[...]
    sh1 = mod_ref[0:1, :]
    sc1 = mod_ref[1:2, :]
    gt1 = mod_ref[2:3, :]
    h = _rms(x, gpre_ref[...]) * (1.0 + sc1) + sh1
    hb = h.astype(BF16)

    def proj(lo, hi):
        return jnp.dot(hb, win_ref[:, lo:hi], preferred_element_type=F32) + bin_ref[:, lo:hi]

    ang = pos_ref[...] * invf_ref[...]
    cos = jnp.cos(ang)
    sin = jnp.sin(ang)
    lane = lax.broadcasted_iota(jnp.int32, (ts, LANES), 1)
    first_half = (lane & (HEAD_DIM // 2)) == 0
    sin_signed = jnp.where(first_half, -sin, sin)

    def rope(t):
        rot = jnp.where(first_half, pltpu.roll(t, LANES - HEAD_DIM // 2, 1), pltpu.roll(t, HEAD_DIM // 2, 1))
        return t * cos + rot * sin_signed

    q = proj(0, qw)
    scale = HEAD_DIM ** -0.5
    qb = jnp.concatenate([rope(q[:, j * LANES:(j + 1) * LANES]) * scale for j in range(qw // LANES)],
                         axis=1).astype(BF16)
    k = proj(qw, qw + kvw)
    kr = jnp.concatenate([rope(k[:, j * LANES:(j + 1) * LANES]) for j in range(kvw // LANES)], axis=1)
    v = proj(qw + kvw, qw + 2 * kvw)
    kext = jnp.concatenate([kprev[...], kr], axis=0)
    vext = jnp.concatenate([vprev[...], v], axis=0)
    kprev[...] = kr[ts - ATTN_BLOCK:, :]
    vprev[...] = v[ts - ATTN_BLOCK:, :]

    lane_e = lax.broadcasted_iota(jnp.int32, (ts + ATTN_BLOCK, LANES), 1)
    low = lane_e < HEAD_DIM

    def head_variants(t):
        lo_ = jnp.where(low, t, 0.0)
        hi_ = jnp.where(low, 0.0, t)
        lo_sw = pltpu.roll(lo_, HEAD_DIM, 1)
        hi_sw = pltpu.roll(hi_, HEAD_DIM, 1)
        return ((lo_.astype(BF16), lo_sw.astype(BF16)), (hi_sw.astype(BF16), hi_.astype(BF16)))

    kvar = []
    vvar = []
    for c in range(kvw // LANES):
        ke, ko = head_variants(kext[:, c * LANES:(c + 1) * LANES])
        ve, vo = head_variants(vext[:, c * LANES:(c + 1) * LANES])
        kvar += [ke, ko]
        vvar += [ve, vo]

    rows2 = 2 * ATTN_BLOCK
    qi = lax.broadcasted_iota(jnp.int32, (rows2, rows2), 0) & (ATTN_BLOCK - 1)
    kj = lax.broadcasted_iota(jnp.int32, (rows2, rows2), 1)
    cur_ok = (kj >= ATTN_BLOCK) & ((kj - ATTN_BLOCK) <= qi)
    prev_ok = (kj < ATTN_BLOCK) & (kj > qi)
    bias_mid = jnp.where(cur_ok | prev_ok, 0.0, NEG_BIG)
    bias_first = jnp.where(cur_ok | (prev_ok & (s_idx > 0)), 0.0, NEG_BIG)
    upper_rows = lax.broadcasted_iota(jnp.int32, (rows2, 1), 0) < ATTN_BLOCK

    group = N_Q_HEADS // N_KV_HEADS
    for blk in range(ts // ATTN_BLOCK):
        r0 = blk * ATTN_BLOCK
        bias = bias_first if blk == 0 else bias_mid
        for g in range(N_KV_HEADS):
            (k_a, k_b), (v_a, v_b) = kvar[g], vvar[g]
            qg = jnp.concatenate([qb[r0:r0 + ATTN_BLOCK, (2 * g) * LANES:(2 * g + 1) * LANES],
                                  qb[r0:r0 + ATTN_BLOCK, (2 * g + 1) * LANES:(2 * g + 2) * LANES]], axis=0)
            og = None
            for half, (kk, vv) in enumerate(((k_a, v_a), (k_b, v_b))):
                sc = lax.dot_general(qg, kk[r0:r0 + rows2, :], (((1,), (1,)), ((), ())),
                                     preferred_element_type=F32) + bias
                sink = jnp.where(upper_rows, sinks_ref[0, group * g + half], sinks_ref[0, group * g + 2 + half])
                m = jnp.maximum(jnp.max(sc, axis=1, keepdims=True), sink)
                p = jnp.exp(sc - m)
                denom = jnp.sum(p, axis=1, keepdims=True) + jnp.exp(sink - m)
                o = jnp.dot(p.astype(BF16), vv[r0:r0 + rows2, :], preferred_element_type=F32) / denom
                og = o if og is None else og + o
            attnbuf[r0:r0 + ATTN_BLOCK, (2 * g) * LANES:(2 * g + 1) * LANES] = og[:ATTN_BLOCK, :]
            attnbuf[r0:r0 + ATTN_BLOCK, (2 * g + 1) * LANES:(2 * g + 2) * LANES] = og[ATTN_BLOCK:, :]

    o3 = qw + 2 * kvw
    lxbuf[SUBLANES:SUBLANES + ts, :] = proj(o3, o3 + d)
    conv = convb_ref[...]
    for tap in range(CONV_WIDTH):
        off = SUBLANES - (CONV_WIDTH - 1) + tap
        conv = conv + lxbuf[off:off + ts, :] * convw_ref[tap:tap + 1, :]
    lxbuf[0:SUBLANES, :] = lxbuf[ts:ts + SUBLANES, :]
    cb = conv.astype(BF16)
    bw = d // LRU_BLOCKS
    rs, is_ = [], []
    for n in range(LRU_BLOCKS):
        ri = jnp.dot(cb[:, n * bw:(n + 1) * bw], wgate_ref[n], preferred_element_type=F32)
        rs.append(ri[:, :bw])
        is_.append(ri[:, bw:])
    r = jax.nn.sigmoid(jnp.concatenate(rs, axis=1) + brg_ref[...])
    i_gate = jax.nn.sigmoid(jnp.concatenate(is_, axis=1) + big_ref[...])
    la = lrua_ref[...]
    softplus = jnp.maximum(la, 0.0) + jnp.log1p(jnp.exp(-jnp.abs(la)))
    a = jnp.exp(-LRU_C * r * softplus)
    u = jnp.sqrt(1.0 - a * a) * (i_gate * conv)
    nlc = d // LANES
    seg = ts // SUBLANES
    pitch = abuf.shape[1] // SUBLANES
    for c in range(nlc):
        for rseg in range(SUBLANES):
            abuf[c, rseg * pitch:rseg * pitch + seg, :] = a[rseg * seg:(rseg + 1) * seg, c * LANES:(c + 1) * LANES]
            hbuf[c, rseg * pitch:rseg * pitch + seg, :] = u[rseg * seg:(rseg + 1) * seg, c * LANES:(c + 1) * LANES]

    h_cols = []
    for c in range(nlc):
        def scan_step(j, carry, c=c):
            hl, prod = carry
            aj = abuf[c, pl.ds(j, SUBLANES, stride=pitch), :]
            uj = hbuf[c, pl.ds(j, SUBLANES, stride=pitch), :]
            hl = aj * hl + uj
            prod = aj * prod
            hbuf[c, pl.ds(j, SUBLANES, stride=pitch), :] = hl
            pbuf[c, pl.ds(j, SUBLANES, stride=pitch), :] = prod
            return hl, prod

        h_end, p_end = lax.fori_loop(0, seg, scan_step,
                                     (jnp.zeros((SUBLANES, LANES), F32), jnp.ones((SUBLANES, LANES), F32)),
                                     unroll=True)
        cstate = hcarry[:, c * LANES:(c + 1) * LANES]
        parts = []
        for rseg in range(SUBLANES):
            rows = slice(rseg * pitch, rseg * pitch + seg)
            parts.append(hbuf[c, rows, :] + pbuf[c, rows, :] * cstate)
            cstate = h_end[rseg:rseg + 1, :] + p_end[rseg:rseg + 1, :] * cstate
        hcarry[:, c * LANES:(c + 1) * LANES] = cstate
        h_cols.append(jnp.concatenate(parts, axis=0))
    h_seq = jnp.concatenate(h_cols, axis=1)

    ly = proj(o3 + d, o3 + 2 * d)
    gelu = 0.5 * ly * (1.0 + jnp.tanh(0.7978845608028654 * (ly + 0.044715 * (ly * ly * ly))))
    lru = h_seq * gelu
    ga = proj(o3 + 2 * d, o3 + 3 * d)
    merged = jax.nn.sigmoid(ga) * attnbuf[...]
    gl = proj(o3 + 3 * d, o3 + 4 * d)
    merged = merged + jax.nn.sigmoid(gl) * lru
    y = jnp.dot(merged.astype(BF16), wout_ref[...], preferred_element_type=F32) + bout_ref[...]
    o_ref[...] = x + gt1 * _rms(y, gpost_ref[...])


def _mixer(x, posb, mod, invf, g_pre, g_post, w_in, b_in, sinks, conv_w, conv_b, w_gate, b_rg, b_ig, lru_a,
           w_out, b_out):
    bsz, seq, d = x.shape
    ts = MIX_CHUNK
    assert seq % ts == 0 and ts % ATTN_BLOCK == 0
    in_w = w_in.shape[1]
    kvw = N_KV_HEADS * HEAD_DIM
    seg = ts // SUBLANES
    assert (seg // SUBLANES) % 2 == 0
    scan_rows = SUBLANES * (seg + SUBLANES)

    def whole(shape):
        return pl.BlockSpec(shape, lambda b, s: (0,) * len(shape))

    return pl.pallas_call(
        _mixer_kernel,
        grid=(bsz, seq // ts),
        in_specs=[
            pl.BlockSpec((None, ts, d), lambda b, s: (b, s, 0)),
            pl.BlockSpec((None, ts, LANES), lambda b, s: (b, s, 0)),
            pl.BlockSpec((None, 6, d), lambda b, s: (b, 0, 0)),
            whole((1, LANES)),
            whole((1, d)),
            whole((1, d)),
            whole((d, in_w)),
            whole((1, in_w)),
            pl.BlockSpec(memory_space=pltpu.SMEM),
            whole((CONV_WIDTH, d)),
            whole((1, d)),
            whole((LRU_BLOCKS, d // LRU_BLOCKS, 2 * d // LRU_BLOCKS)),
            whole((1, d)),
            whole((1, d)),
            whole((1, d)),
            whole((d, d)),
            whole((1, d)),
        ],
        out_specs=pl.BlockSpec((None, ts, d), lambda b, s: (b, s, 0)),
        out_shape=jax.ShapeDtypeStruct((bsz, seq, d), F32),
        scratch_shapes=[
            pltpu.VMEM((ATTN_BLOCK, kvw), F32),
            pltpu.VMEM((ATTN_BLOCK, kvw), F32),
            pltpu.VMEM((ts + SUBLANES, d), F32),
            pltpu.VMEM((1, d), F32),
            pltpu.VMEM((d // LANES, scan_rows, LANES), F32),
            pltpu.VMEM((d // LANES, scan_rows, LANES), F32),
            pltpu.VMEM((d // LANES, scan_rows, LANES), F32),
            pltpu.VMEM((ts, d), F32),
        ],
        compiler_params=pltpu.CompilerParams(dimension_semantics=("arbitrary", "arbitrary"),
                                             vmem_limit_bytes=VMEM_LIMIT),
        name="mixer",
    )(x, posb, mod, invf, g_pre, g_post, w_in, b_in, sinks, conv_w, conv_b, w_gate, b_rg, b_ig, lru_a,
      w_out, b_out)


def _router_kernel(x_ref, mod_ref, g_ref, wr_hi_ref, wr_lo_ref, br_ref, h_ref, ti_ref, tw_ref, cnt_ref):
    tr = x_ref.shape[0]

    @pl.when(pl.program_id(0) == 0)
    def _():
        cnt_ref[...] = jnp.zeros_like(cnt_ref)

    sh2 = mod_ref[3:4, :]
    sc2 = mod_ref[4:5, :]
    h = _rms(x_ref[...], g_ref[...]) * (1.0 + sc2) + sh2
    for j in range(h.shape[1] // LANES):
        h_ref[pl.ds(j, tr, stride=SUBLANES), :] = h[:, j * LANES:(j + 1) * LANES]
    h_hi = h.astype(BF16)
    h_lo = (h - h_hi.astype(F32)).astype(BF16)
    logits = (jnp.dot(h_hi, wr_hi_ref[...], preferred_element_type=F32)
              + jnp.dot(h_hi, wr_lo_ref[...], preferred_element_type=F32)
              + jnp.dot(h_lo, wr_hi_ref[...], preferred_element_type=F32)) + br_ref[...]
    lane = lax.broadcasted_iota(jnp.int32, (tr, LANES), 1)
    work = logits
    vals, idxs = [], []
    for _ in range(TOP_K):
        m = jnp.max(work, axis=1, keepdims=True)
        idx = jnp.min(jnp.where(work == m, lane, LANES), axis=1, keepdims=True)
        vals.append(m)
        idxs.append(idx)
        work = jnp.where(lane == idx, -jnp.inf, work)
    es = [jnp.exp(vk - vals[0]) for vk in vals]
    tot = es[0] + es[1] + es[2] + es[3]
    hot = [lane == idxs[kk] for kk in range(TOP_K)]
    multihot = (hot[0] | hot[1] | hot[2] | hot[3]).astype(BF16)
    before = (lax.broadcasted_iota(jnp.int32, (tr, tr), 1) < lax.broadcasted_iota(jnp.int32, (tr, tr), 0))
    seen = jnp.dot(before.astype(BF16), multihot, preferred_element_type=F32) + cnt_ref[...]
    cnt_ref[...] = cnt_ref[...] + jnp.sum(multihot.astype(F32), axis=0, keepdims=True)
    tw = jnp.zeros((tr, LANES), F32)
    ti = jnp.zeros((tr, LANES), jnp.int32)
    for kk in range(TOP_K):
        rank = jnp.sum(jnp.where(hot[kk], seen, 0.0), axis=1, keepdims=True).astype(jnp.int32)
        tw = jnp.where(lane == kk, es[kk] / tot, tw)
        ti = jnp.where(lane == kk, idxs[kk], ti)
        ti = jnp.where(lane == TOP_K + kk, rank, ti)
    tw_ref[...] = tw
    ti_ref[...] = ti


def _router(x1, mod, g, wr_hi, wr_lo, br):
    bsz, seq, d = x1.shape
    tr = ROUTE_TILE
    assert seq % tr == 0
    per = seq // tr
    t = bsz * seq
    x2 = x1.reshape(t, d)
    return pl.pallas_call(
        _router_kernel,
        grid=(t // tr,),
        in_specs=[
            pl.BlockSpec((tr, d), lambda i: (i, 0)),
            pl.BlockSpec((None, 6, d), lambda i: (i // per, 0, 0)),
            pl.BlockSpec((1, d), lambda i: (0, 0)),
            pl.BlockSpec((d, LANES), lambda i: (0, 0)),
            pl.BlockSpec((d, LANES), lambda i: (0, 0)),
            pl.BlockSpec((1, LANES), lambda i: (0, 0)),
        ],
        out_specs=[
            pl.BlockSpec((tr * d // LANES, LANES), lambda i: (i, 0)),
            pl.BlockSpec((tr, LANES), lambda i: (i, 0)),
            pl.BlockSpec((tr, LANES), lambda i: (i, 0)),
            pl.BlockSpec((1, LANES), lambda i: (0, 0)),
        ],
        out_shape=[
            jax.ShapeDtypeStruct((t * d // LANES, LANES), F32),
            jax.ShapeDtypeStruct((t, LANES), jnp.int32),
            jax.ShapeDtypeStruct((t, LANES), F32),
            jax.ShapeDtypeStruct((1, LANES), F32),
        ],
        compiler_params=pltpu.CompilerParams(dimension_semantics=("arbitrary",), vmem_limit_bytes=VMEM_LIMIT),
        name="router",
    )(x2, mod, g, wr_hi, wr_lo, br)


ROW_COPY_UNROLL = 8


def _token_tile(ref, t):
    return ref.at[pl.ds(pl.multiple_of(t * SUBLANES, SUBLANES), SUBLANES)]


def _dispatch_kernel(pz_ref, pos_hbm, h_ref, xs_hbm, pos_smem, psem, zbuf, zsem, sem):
    i = pl.program_id(0)
    n = pl.num_programs(0)
    tm = h_ref.shape[0] // SUBLANES
    per = TOP_K * tm

    def pos_copy(j):
        return pltpu.make_async_copy(pos_hbm.at[j], pos_smem.at[pl.ds((j % 2) * per, per)], psem.at[j % 2])

    def zero_copy(e):
        return pltpu.make_async_copy(zbuf, xs_hbm.at[pl.ds(pl.multiple_of(pz_ref[e] * SUBLANES, SUBLANES),
                                                           zbuf.shape[0])], zsem)

    @pl.when(i == 0)
    def _():
        pos_copy(0).start()
        zbuf[...] = jnp.zeros_like(zbuf)
        for e in range(N_EXPERTS):
            @pl.when(pz_ref[e] >= 0)
            def _():
                zero_copy(e).start()
        for e in range(N_EXPERTS):
            @pl.when(pz_ref[e] >= 0)
            def _():
                zero_copy(e).wait()

        def tail_copy(j):
            return pltpu.make_async_copy(zbuf, xs_hbm.at[pl.ds(pl.multiple_of(j * zbuf.shape[0], SUBLANES),
                                                               zbuf.shape[0])], zsem)

        def fill(j, c):
            tail_copy(j).start()
            tail_copy(j).wait()
            return c

        lax.fori_loop(pz_ref[N_EXPERTS], xs_hbm.shape[0] // zbuf.shape[0], fill, 0)

    pos_copy(i).wait()

    @pl.when(i + 1 < n)
    def _():
        pos_copy(i + 1).start()

    off = (i % 2) * per

    def body(t, c):
        for kk in range(TOP_K):
            pltpu.make_async_copy(_token_tile(h_ref, t), _token_tile(xs_hbm, pos_smem[off + TOP_K * t + kk]),
                                  sem).start()
        return c

    lax.fori_loop(0, tm, body, 0, unroll=ROW_COPY_UNROLL // 2)
    for _ in range(TOP_K):
        pltpu.make_async_copy(h_ref, xs_hbm.at[pl.ds(0, h_ref.shape[0])], sem).wait()


def _dispatch(pz, pos, h2, n_slots):
    t = pos.shape[0]
    tm = DISPATCH_TILE
    assert t % tm == 0
    pos2 = pos.reshape(t // tm, tm * TOP_K)
    grid_spec = pltpu.PrefetchScalarGridSpec(
        num_scalar_prefetch=1,
        grid=(t // tm,),
        in_specs=[
            pl.BlockSpec(memory_space=pl.ANY),
            pl.BlockSpec((tm * SUBLANES, LANES), lambda i, pz: (i, 0)),
        ],
        out_specs=pl.BlockSpec(memory_space=pl.ANY),
        scratch_shapes=[
            pltpu.SMEM((2 * tm * TOP_K,), jnp.int32),
            pltpu.SemaphoreType.DMA((2,)),
            pltpu.VMEM((EXPERT_ROWS * SUBLANES, LANES), F32),
            pltpu.SemaphoreType.DMA(()),
            pltpu.SemaphoreType.DMA(()),
        ],
    )
    return pl.pallas_call(
        _dispatch_kernel,
        grid_spec=grid_spec,
        out_shape=jax.ShapeDtypeStruct((n_slots * SUBLANES, LANES), F32),
        compiler_params=pltpu.CompilerParams(dimension_semantics=("arbitrary",), vmem_limit_bytes=VMEM_LIMIT),
        name="dispatch",
    )(pz, pos2, h2)


def _expert_kernel(be_ref, nu_ref, xs_ref, w1_ref, b1_ref, w2_ref, b2_ref, o_ref):
    del be_ref
    i = pl.program_id(0)
    rows = xs_ref.shape[0] // SUBLANES
    f = w2_ref.shape[0]
    nlc = w2_ref.shape[1] // LANES

    @pl.when(i < nu_ref[0])
    def _():
        xv = jnp.concatenate([xs_ref[pl.ds(j, rows, stride=SUBLANES), :] for j in range(nlc)],
                             axis=1).astype(BF16)
        gu = jnp.dot(xv, w1_ref[...], preferred_element_type=F32) + b1_ref[...]
        g = jnp.minimum(gu[:, :f], SWIGLU_LIMIT)
        u = jnp.clip(gu[:, f:], -SWIGLU_LIMIT, SWIGLU_LIMIT)
        act = (g * jax.nn.sigmoid(SWIGLU_ALPHA * g) * (u + 1.0)).astype(BF16)
        out = jnp.dot(act, w2_ref[...], preferred_element_type=F32) + b2_ref[...]
        for j in range(nlc):
            o_ref[pl.ds(j, rows, stride=SUBLANES), :] = out[:, j * LANES:(j + 1) * LANES]

    @pl.when(i >= nu_ref[0])
    def _():
        o_ref[...] = jnp.zeros_like(o_ref)


def _experts(blk_expert, n_used, xs, w1, b1, w2, b2):
    rows = EXPERT_ROWS
    n_blocks = xs.shape[0] // (rows * SUBLANES)
    d, f2 = w1.shape[1], w1.shape[2]
    f = w2.shape[1]
    assert d == SUBLANES * LANES
    grid_spec = pltpu.PrefetchScalarGridSpec(
        num_scalar_prefetch=2,
        grid=(n_blocks,),
        in_specs=[
            pl.BlockSpec((rows * SUBLANES, LANES), lambda i, be, nu: (i, 0)),
            pl.BlockSpec((None, d, f2), lambda i, be, nu: (be[i], 0, 0)),
            pl.BlockSpec((None, 1, f2), lambda i, be, nu: (be[i], 0, 0)),
            pl.BlockSpec((None, f, d), lambda i, be, nu: (be[i], 0, 0)),
            pl.BlockSpec((None, 1, d), lambda i, be, nu: (be[i], 0, 0)),
        ],
        out_specs=pl.BlockSpec((rows * SUBLANES, LANES), lambda i, be, nu: (i, 0)),
    )
    return pl.pallas_call(
        _expert_kernel,
        grid_spec=grid_spec,
        out_shape=jax.ShapeDtypeStruct(xs.shape, F32),
        compiler_params=pltpu.CompilerParams(dimension_semantics=("arbitrary",), vmem_limit_bytes=VMEM_LIMIT),
        name="experts",
    )(blk_expert, n_used, xs, w1, b1, w2, b2)


def _combine_kernel(pos_hbm, x_ref, mod_ref, g_ref, tw_ref, ys_hbm, o_ref, pos_smem, psem, buf_a, buf_b, gsem):
    s = pl.program_id(0)
    ns = pl.num_programs(0)
    tc = x_ref.shape[0] // 2
    d = x_ref.shape[1]
    per = TOP_K * tc

    def pos_copy(j):
        return pltpu.make_async_copy(pos_hbm.at[pl.ds((2 * j + 1) * per, 2 * per)],
                                     pos_smem.at[pl.ds((j % 2) * 2 * per, 2 * per)], psem.at[j % 2])

    def gather(buf, sem_i, off):
        def body(t, c):
            for kk in range(TOP_K):
                pltpu.make_async_copy(_token_tile(ys_hbm, pos_smem[off + TOP_K * t + kk]),
                                      _token_tile(buf.at[kk], t), gsem.at[sem_i]).start()
            return c

        lax.fori_loop(0, tc, body, 0, unroll=ROW_COPY_UNROLL // 2)

    def wait_gather(buf, sem_i):
        for kk in range(TOP_K):
            pltpu.make_async_copy(ys_hbm.at[pl.ds(0, tc * SUBLANES)], buf.at[kk], gsem.at[sem_i]).wait()

    def combine(buf, r0):
        tw = tw_ref[r0:r0 + tc, :]
        cols = []
        for j in range(d // LANES):
            acc = None
            for kk in range(TOP_K):
                term = buf[kk, pl.ds(j, tc, stride=SUBLANES), :] * tw[:, kk:kk + 1]
                acc = term if acc is None else acc + term
            cols.append(acc)
        y = jnp.concatenate(cols, axis=1)
        o_ref[r0:r0 + tc, :] = x_ref[r0:r0 + tc, :] + mod_ref[5:6, :] * _rms(y, g_ref[...])

    @pl.when(s == 0)
    def _():
        first = pltpu.make_async_copy(pos_hbm.at[pl.ds(0, per)], pos_smem.at[pl.ds(2 * per, per)], psem.at[1])
        first.start()
        first.wait()
        gather(buf_a, 0, 2 * per)
        pos_copy(0).start()

    pos_copy(s).wait()

    @pl.when(s + 1 < ns)
    def _():
        pos_copy(s + 1).start()

    off = (s % 2) * 2 * per
    gather(buf_b, 1, off)
    wait_gather(buf_a, 0)
    combine(buf_a, 0)
    gather(buf_a, 0, off + per)
    wait_gather(buf_b, 1)
    combine(buf_b, tc)

    @pl.when(s == ns - 1)
    def _():
        wait_gather(buf_a, 0)


def _combine(pos, x1, mod, g, tw, ys):
    bsz, seq, d = x1.shape
    tc = COMB_TILE
    assert seq % (2 * tc) == 0
    per = seq // (2 * tc)
    t = bsz * seq
    n_steps = t // (2 * tc)
    pos_flat = jnp.concatenate([pos.reshape(-1), jnp.zeros((2 * TOP_K * tc,), jnp.int32)])
    out = pl.pallas_call(
        _combine_kernel,
        grid=(n_steps,),
        in_specs=[
            pl.BlockSpec(memory_space=pl.ANY),
            pl.BlockSpec((2 * tc, d), lambda i: (i, 0)),
            pl.BlockSpec((None, 6, d), lambda i: (i // per, 0, 0)),
            pl.BlockSpec((1, d), lambda i: (0, 0)),
            pl.BlockSpec((2 * tc, LANES), lambda i: (i, 0)),
            pl.BlockSpec(memory_space=pl.ANY),
        ],
        out_specs=pl.BlockSpec((2 * tc, d), lambda i: (i, 0)),
        out_shape=jax.ShapeDtypeStruct((t, d), F32),
        scratch_shapes=[
            pltpu.SMEM((2 * 2 * TOP_K * tc,), jnp.int32),
            pltpu.SemaphoreType.DMA((2,)),
            pltpu.VMEM((TOP_K, tc * SUBLANES, LANES), F32),
            pltpu.VMEM((TOP_K, tc * SUBLANES, LANES), F32),
            pltpu.SemaphoreType.DMA((2,)),
        ],
        compiler_params=pltpu.CompilerParams(dimension_semantics=("arbitrary",), vmem_limit_bytes=VMEM_LIMIT),
        name="combine",
    )(pos_flat, x1.reshape(t, d), mod, g, tw, ys)
    return out.reshape(bsz, seq, d)


def _slot_tables(top_i, rank, counts, n_blocks):
    rows = EXPERT_ROWS
    pcounts = (counts + rows - 1) // rows * rows
    pends = jnp.cumsum(pcounts)
    pstarts = pends - pcounts
    onehot = top_i[:, :, None] == jnp.arange(N_EXPERTS, dtype=jnp.int32)[None, None, :]
    pos = rank + jnp.sum(jnp.where(onehot, pstarts[None, None, :], 0), axis=-1)
    blk_expert = jnp.minimum(
        jnp.searchsorted(pends, jnp.arange(n_blocks, dtype=jnp.int32) * rows, side="right"), N_EXPERTS - 1
    ).astype(jnp.int32)
    n_used = (pends[-1] // rows).astype(jnp.int32).reshape(1)
    last_block = jnp.where(pcounts > 0, pends - rows, -1).astype(jnp.int32)
    return pos.astype(jnp.int32), blk_expert, n_used, last_block


def kernel(x, c, positions, w_ada, b_ada, g_pre_mix, g_post_mix, w_in, b_in, sinks, conv_w, conv_b, w_rg, b_rg,
           w_ig, b_ig, lru_a, w_out, b_out, g_pre_ffn, g_post_ffn, w_router, b_router, w_e1, b_e1, w_e2, b_e2):
    bsz, seq, d = x.shape
    depth = w_ada.shape[0]
    t = bsz * seq
    inv_freq = ROPE_THETA ** (-jnp.arange(0, HEAD_DIM, 2, dtype=F32) / HEAD_DIM)
    invf = jnp.tile(inv_freq, LANES // (HEAD_DIM // 2)).reshape(1, LANES)
    posb = jnp.broadcast_to(positions.astype(F32)[:, :, None], (bsz, seq, LANES))
    n_blocks = (t * TOP_K) // EXPERT_ROWS + N_EXPERTS
    n_slots = n_blocks * EXPERT_ROWS
    row = lambda a: a.reshape(1, -1)
    for l in range(depth):
        mod = _ada(c, w_ada[l], b_ada[l]).reshape(bsz, 6, d)
        w_gate = jnp.concatenate([w_rg[l], w_ig[l]], axis=-1).astype(BF16)
        x = _mixer(x, posb, mod, invf, row(g_pre_mix[l]), row(g_post_mix[l]), w_in[l].astype(BF16), row(b_in[l]),
                   row(sinks[l]), conv_w[l], row(conv_b[l]), w_gate, row(b_rg[l]), row(b_ig[l]), row(lru_a[l]),
                   w_out[l].astype(BF16), row(b_out[l]))
        wr = jnp.pad(w_router[l].astype(F32), ((0, 0), (0, LANES - N_EXPERTS)))
        wr_hi = wr.astype(BF16)
        wr_lo = (wr - wr_hi.astype(F32)).astype(BF16)
        br = jnp.pad(b_router[l].astype(F32), (0, LANES - N_EXPERTS), constant_values=NEG_BIG).reshape(1, LANES)
        h2, ti, tw, cnt = _router(x, mod, row(g_pre_ffn[l]), wr_hi, wr_lo, br)
        counts = cnt[0, :N_EXPERTS].astype(jnp.int32)
        pos, blk_expert, n_used, last_block = _slot_tables(ti[:, :TOP_K], ti[:, TOP_K:2 * TOP_K], counts, n_blocks)
        xs = _dispatch(jnp.concatenate([last_block, n_used]), pos, h2, n_slots)
        ys = _experts(blk_expert, n_used, xs, w_e1[l].astype(BF16), b_e1[l][:, None, :], w_e2[l].astype(BF16),
                      b_e2[l][:, None, :])
        x = _combine(pos, x, mod, row(g_post_ffn[l]), tw, ys)
    return x
```
